```python
import math
import jax, jax.numpy as jnp
from jax import lax
import numpy as np

D_MODEL = 1024
BATCH = 4
SEQ = 8192
DEPTH = 2

D_MIX = D_MODEL
A_WIDTH = D_MIX // 2
A_HEAD_DIM = 128
A_HEADS = A_WIDTH // A_HEAD_DIM
B_WIDTH = D_MIX - A_WIDTH
B_GROUPS = 4
B_GROUP_DIM = B_WIDTH // B_GROUPS
D_IN = 4 * A_WIDTH + 2 * B_WIDTH
D_FF = int(math.ceil(8 * D_MODEL / 3 / 128)) * 128
SPATIAL_CHUNK = 128
RECUR_CHUNK = 64
FFN_RES = 0.5
EPS = 1e-6
F_MIN = 1e-20

kernel_name = "hybrid_hgrn2_gmlp_macaron"


def rmsnorm(x, g):
    xf = x.astype(jnp.float32)
    y = xf * lax.rsqrt(jnp.mean(xf * xf, axis=-1, keepdims=True) + EPS)
    return (y * g.astype(jnp.float32)).astype(x.dtype)


def layernorm(x, g, b):
    xf = x.astype(jnp.float32)
    mu = jnp.mean(xf, axis=-1, keepdims=True)
    var = jnp.mean(jnp.square(xf - mu), axis=-1, keepdims=True)
    y = (xf - mu) * lax.rsqrt(var + EPS)
    return (y * g.astype(jnp.float32) + b.astype(jnp.float32)).astype(x.dtype)


def swiglu_ffn(h, w_gate, w_up, w_down):
    return (jax.nn.silu(h @ w_gate) * (h @ w_up)) @ w_down


def hgrn2_mixer(q_in, f_in, i_in, g_in, lb, norm_g):
    bsz, seq, _ = q_in.shape
    dt = q_in.dtype
    n_chunks = seq // RECUR_CHUNK
    zf = f_in.astype(jnp.float32)
    lb32 = lb.astype(jnp.float32)
    f = lb32 + (1.0 - lb32) * jax.nn.sigmoid(zf)
    log_f = jnp.log(jnp.maximum(f, F_MIN))
    k_in = (1.0 - lb32) * jax.nn.sigmoid(-zf)

    def to_chunks(t):
        return t.astype(jnp.float32).reshape(bsz, n_chunks, RECUR_CHUNK, A_HEADS, A_HEAD_DIM).transpose(1, 0, 3, 2, 4)

    causal = jnp.tril(jnp.ones((RECUR_CHUNK, RECUR_CHUNK), dtype=bool))[:, :, None]

    def step(state, inp):
        q, k, lf, v = inp
        a = jnp.cumsum(lf, axis=2)
        rel = a[:, :, :, None, :] - a[:, :, None, :, :]
        decay = jnp.where(causal, jnp.exp(jnp.where(causal, rel, 0.0)), 0.0)
        scores = jnp.einsum('bhtk,bhsk,bhtsk->bhts', q, k, decay)
        o = jnp.einsum('bhts,bhsv->bhtv', scores, v)
        o = o + jnp.einsum('bhtk,bhkv->bhtv', q * jnp.exp(a), state)
        a_last = a[:, :, -1:, :]
        k_dec = k * jnp.exp(a_last - a)
        new_state = jnp.exp(a_last[:, :, 0, :])[..., None] * state + jnp.einsum('bhsk,bhsv->bhkv', k_dec, v)
        return new_state, o

    s0 = jnp.zeros((bsz, A_HEADS, A_HEAD_DIM, A_HEAD_DIM), jnp.float32)
    _, o = lax.scan(step, s0, (to_chunks(q_in), to_chunks(k_in), to_chunks(log_f), to_chunks(i_in)))
    o = o.transpose(1, 0, 3, 2, 4).reshape(bsz, seq, A_HEADS, A_HEAD_DIM)
    o = o * lax.rsqrt(jnp.mean(o * o, axis=-1, keepdims=True) + EPS)
    o = o.reshape(bsz, seq, A_WIDTH) * norm_g.astype(jnp.float32)
    o = o * jax.nn.silu(g_in.astype(jnp.float32))
    return o.astype(dt)


def gmlp_mixer(u_in, v_in, ln_g, ln_b, w_spatial, b_spatial):
    bsz, seq, _ = u_in.shape
    u = jax.nn.gelu(u_in)
    v = layernorm(jax.nn.gelu(v_in), ln_g, ln_b)
    vr = v.reshape(bsz, seq // SPATIAL_CHUNK, SPATIAL_CHUNK, B_GROUPS, B_GROUP_DIM)
    w = jnp.where(jnp.tril(jnp.ones((SPATIAL_CHUNK, SPATIAL_CHUNK), dtype=bool))[None], w_spatial, 0.0).astype(v.dtype)
    s = jnp.einsum('gts,bnsgc->bntgc', w, vr) + b_spatial.T[None, None, :, :, None].astype(v.dtype)
    return u * s.reshape(bsz, seq, B_WIDTH)


def setup_inputs(seed: int = 0) -> dict:
    key = jax.random.key(seed)
    ks = jax.random.split(key, 24)
    f32 = jnp.float32

    def nrm(k, shape, scale):
        return jax.random.normal(k, shape, f32) * scale

    def gain(k, shape):
        return 1.0 + 0.01 * jax.random.normal(k, shape, f32)

    return {
        "x": nrm(ks[0], (BATCH, SEQ, D_MODEL), 1.0),
        "norm_ffn1": gain(ks[1], (DEPTH, D_MODEL)),
        "ffn1_w_gate": nrm(ks[2], (DEPTH, D_MODEL, D_FF), D_MODEL ** -0.5),
        "ffn1_w_up": nrm(ks[3], (DEPTH, D_MODEL, D_FF), D_MODEL ** -0.5),
        "ffn1_w_down": nrm(ks[4], (DEPTH, D_FF, D_MODEL), D_FF ** -0.5),
        "norm_mix": gain(ks[5], (DEPTH, D_MODEL)),
        "w_in": nrm(ks[6], (DEPTH, D_MODEL, D_IN), D_MODEL ** -0.5),
        "lb_param": nrm(ks[7], (DEPTH, A_WIDTH), 0.1),
        "hgrn_norm": gain(ks[8], (DEPTH, A_WIDTH)),
        "ln_v_gain": gain(ks[9], (DEPTH, B_WIDTH)),
        "ln_v_bias": nrm(ks[10], (DEPTH, B_WIDTH), 0.01),
        "w_spatial": nrm(ks[11], (DEPTH, B_GROUPS, SPATIAL_CHUNK, SPATIAL_CHUNK), 0.5 * SPATIAL_CHUNK ** -0.5),
        "b_spatial": gain(ks[12], (DEPTH, B_GROUPS, SPATIAL_CHUNK)),
        "w_out": nrm(ks[13], (DEPTH, D_MIX, D_MODEL), D_MIX ** -0.5),
        "norm_ffn2": gain(ks[14], (DEPTH, D_MODEL)),
        "ffn2_w_gate": nrm(ks[15], (DEPTH, D_MODEL, D_FF), D_MODEL ** -0.5),
        "ffn2_w_up": nrm(ks[16], (DEPTH, D_MODEL, D_FF), D_MODEL ** -0.5),
        "ffn2_w_down": nrm(ks[17], (DEPTH, D_FF, D_MODEL), D_FF ** -0.5),
        "norm_final": gain(ks[18], (D_MODEL,)),
    }


def reference(x, norm_ffn1, ffn1_w_gate, ffn1_w_up, ffn1_w_down, norm_mix, w_in, lb_param, hgrn_norm,
              ln_v_gain, ln_v_bias, w_spatial, b_spatial, w_out, norm_ffn2, ffn2_w_gate, ffn2_w_up,
              ffn2_w_down, norm_final):
    p = jax.nn.softmax(lb_param.astype(jnp.float32), axis=0)
    lbs = jnp.cumsum(p, axis=0) - p[0]
    split_idx = [A_WIDTH, 2 * A_WIDTH, 3 * A_WIDTH, 4 * A_WIDTH, 4 * A_WIDTH + B_WIDTH]
    for l in range(DEPTH):
        x = x + FFN_RES * swiglu_ffn(rmsnorm(x, norm_ffn1[l]), ffn1_w_gate[l], ffn1_w_up[l], ffn1_w_down[l])
        h = rmsnorm(x, norm_mix[l])
        q_a, f_a, i_a, g_a, u_b, v_b = jnp.split(h @ w_in[l], split_idx, axis=-1)
        o_a = hgrn2_mixer(q_a, f_a, i_a, g_a, lbs[l], hgrn_norm[l])
        o_b = gmlp_mixer(u_b, v_b, ln_v_gain[l], ln_v_bias[l], w_spatial[l], b_spatial[l])
        x = x + jnp.concatenate([o_a, o_b], axis=-1) @ w_out[l]
        x = x + FFN_RES * swiglu_ffn(rmsnorm(x, norm_ffn2[l]), ffn2_w_gate[l], ffn2_w_up[l], ffn2_w_down[l])
    return rmsnorm(x, norm_final)
```

```python
import functools
import math

import jax
import jax.numpy as jnp
from jax import lax
from jax.experimental import pallas as pl
from jax.experimental.pallas import tpu as pltpu

D_MODEL = 1024
DEPTH = 2
A_WIDTH = 512
HEAD_DIM = 128
N_HEADS = A_WIDTH // HEAD_DIM
B_WIDTH = 512
N_GROUPS = 4
GROUP_DIM = B_WIDTH // N_GROUPS
D_IN = 4 * A_WIDTH + 2 * B_WIDTH
D_FF = int(math.ceil(8 * D_MODEL / 3 / 128)) * 128
SPATIAL_CHUNK = 128
RECUR_CHUNK = 64
FFN_RES = 0.5
EPS = 1e-6
F_MIN = 1e-20

SUBLANES = 8
DIAG = SUBLANES
LEVELS = (32, 16, 8)
FFN_TILE = 512
MIX_TILE = 512
VMEM_LIMIT_BYTES = 56 * 1024 * 1024

F32 = jnp.float32
BF16 = jnp.bfloat16


def _rmsnorm(x, g):
    return x * lax.rsqrt(jnp.mean(x * x, axis=-1, keepdims=True) + EPS) * g


def _dot(a, b):
    return jnp.dot(a, b, preferred_element_type=F32)


def _dot_nt(a, b):
    return lax.dot_general(a, b, (((1,), (1,)), ((), ())), preferred_element_type=F32)


def _dot_tn(a, b):
    return lax.dot_general(a, b, (((0,), (0,)), ((), ())), preferred_element_type=F32)


def _ffn_kernel(x_ref, g_ref, wg_ref, wu_ref, wd_ref, gf_ref, o_ref, *, final):
    x = x_ref[...]
    h = _rmsnorm(x, g_ref[...]).astype(BF16)
    gate = _dot(h, wg_ref[...])
    up = _dot(h, wu_ref[...])
    act = (gate * jax.nn.sigmoid(gate) * up).astype(BF16)
    out = x + FFN_RES * _dot(act, wd_ref[...])
    if final:
        out = _rmsnorm(out, gf_ref[...])
    o_ref[...] = out


def _resident(shape):
    return pl.BlockSpec(shape, lambda *_: (0,) * len(shape), pipeline_mode=pl.Buffered(1))


def _ffn(x2d, g, wg, wu, wd, gf, *, final):
    m = x2d.shape[0]
    tile = pl.BlockSpec((FFN_TILE, D_MODEL), lambda i: (i, 0))
    return pl.pallas_call(
        functools.partial(_ffn_kernel, final=final),
        grid=(m // FFN_TILE,),
        in_specs=[tile, _resident((1, D_MODEL)), _resident((D_MODEL, D_FF)), _resident((D_MODEL, D_FF)),
                  _resident((D_FF, D_MODEL)), _resident((1, D_MODEL))],
        out_specs=tile,
        out_shape=jax.ShapeDtypeStruct(x2d.shape, F32),
        compiler_params=pltpu.CompilerParams(dimension_semantics=("arbitrary",),
                                             vmem_limit_bytes=VMEM_LIMIT_BYTES),
        name="ffn_final" if final else "ffn",
    )(x2d, g, wg, wu, wd, gf)


def _split3(x):
    hi = x.astype(BF16)
    r1 = x - hi.astype(F32)
    mid = r1.astype(BF16)
    lo = (r1 - mid.astype(F32)).astype(BF16)
    return hi, mid, lo


def _hgrn2_head(q, z, v, lb, st, tril_c, level_masks, diag_mask, lane_ids):
    c = RECUR_CHUNK
    e = jnp.exp(-jnp.abs(z))
    inv = 1.0 / (1.0 + e)
    sig_pos = jnp.where(z >= 0, 1.0, e) * inv
    sig_neg = jnp.where(z >= 0, e, 1.0) * inv
    f = lb + (1.0 - lb) * sig_pos
    lf = jnp.log(jnp.maximum(f, F_MIN))
    kk = (1.0 - lb) * sig_neg

    hi, mid, lo = _split3(lf)
    a = _dot(tril_c, hi) + _dot(tril_c, mid) + _dot(tril_c, lo)

    o = _dot_nt((q * jnp.exp(a)).astype(BF16), st.astype(BF16))

    p = jnp.zeros((c, c), F32)
    for m, mask in zip(LEVELS, level_masks):
        ref = jnp.concatenate(
            [jnp.broadcast_to(a[(2 * j + 1) * m - 1:(2 * j + 1) * m, :], (2 * m, HEAD_DIM))
             for j in range(c // (2 * m))], axis=0)
        x = jnp.exp(-jnp.abs(a - ref))
        sc = _dot_nt((q * x).astype(BF16), (kk * x).astype(BF16))
        p = p + jnp.where(mask, sc, 0.0)

    blocks = []
    for bi in range(c // DIAG):
        r0 = bi * DIAG
        qb = q[r0:r0 + DIAG, :]
        ab = a[r0:r0 + DIAG, :]
        pb = jnp.zeros((DIAG, HEAD_DIM), F32)
        for s in range(DIAG):
            a_s = a[r0 + s:r0 + s + 1, :]
            k_s = kk[r0 + s:r0 + s + 1, :]
            w = qb * jnp.exp(jnp.minimum(ab - a_s, 0.0)) * k_s
            col = jnp.sum(w, axis=-1, keepdims=True)
            pb = jnp.where(lane_ids == r0 + s, col, pb)
        blocks.append(pb)
    pd = jnp.concatenate(blocks, axis=0)
    p = p + jnp.where(diag_mask, pd[:, :c], 0.0)

    o = o + _dot(p.astype(BF16), v.astype(BF16))

    a_last = a[c - 1:c, :]
    k_dec = kk * jnp.exp(a_last - a)
    st_new = st * jnp.exp(a_last) + _dot_tn(v.astype(BF16), k_dec.astype(BF16))
    return o, st_new


def _mixer_kernel(x_ref, gmix_ref, win_ref, lbp_ref, hnorm_ref, lng_ref, lnb_ref, wsp_ref, bsp_ref, wout_ref,
                  o_ref, state_ref, proj_ref, mix_ref, *, layer):
    c = RECUR_CHUNK

    @pl.when(pl.program_id(1) == 0)
    def _():
        state_ref[...] = jnp.zeros(state_ref.shape, F32)

    x = x_ref[0]
    h = _rmsnorm(x, gmix_ref[...]).astype(BF16)
    proj_ref[...] = _dot(h, win_ref[...])

    lp = lbp_ref[...]
    ex = jnp.exp(lp - jnp.max(lp, axis=0, keepdims=True))
    prob = ex / jnp.sum(ex, axis=0, keepdims=True)
    lbs = jnp.sum(prob[:layer + 1, :], axis=0, keepdims=True) - prob[0:1, :]

    rows = lax.broadcasted_iota(jnp.int32, (c, c), 0)
    cols = lax.broadcasted_iota(jnp.int32, (c, c), 1)
    tril_c = (rows >= cols).astype(BF16)
    level_masks = [((rows // (2 * m)) == (cols // (2 * m))) & ((rows // m) % 2 == 1) & ((cols // m) % 2 == 0)
                   for m in LEVELS]
    diag_mask = ((rows // DIAG) == (cols // DIAG)) & (rows >= cols)
    lane_ids = lax.broadcasted_iota(jnp.int32, (DIAG, HEAD_DIM), 1)
    hnorm = hnorm_ref[...]

    def chunk_body(ci, carry):
        r0 = pl.multiple_of(ci * c, c)
        for hd in range(N_HEADS):
            lo, hi = hd * HEAD_DIM, (hd + 1) * HEAD_DIM
            q = proj_ref[pl.ds(r0, c), lo:hi]
            z = proj_ref[pl.ds(r0, c), A_WIDTH + lo:A_WIDTH + hi]
            v = proj_ref[pl.ds(r0, c), 2 * A_WIDTH + lo:2 * A_WIDTH + hi]
            g = proj_ref[pl.ds(r0, c), 3 * A_WIDTH + lo:3 * A_WIDTH + hi]
            o, st_new = _hgrn2_head(q, z, v, lbs[:, lo:hi], state_ref[hd], tril_c, level_masks, diag_mask,
                                    lane_ids)
            state_ref[hd] = st_new
            o = o * lax.rsqrt(jnp.mean(o * o, axis=-1, keepdims=True) + EPS) * hnorm[:, lo:hi]
            o = o * (g * jax.nn.sigmoid(g))
            mix_ref[pl.ds(r0, c), lo:hi] = o.astype(BF16)
        return carry

    lax.fori_loop(0, MIX_TILE // c, chunk_body, 0)

    p = SPATIAL_CHUNK
    prow = lax.broadcasted_iota(jnp.int32, (p, p), 0)
    pcol = lax.broadcasted_iota(jnp.int32, (p, p), 1)
    w_tril = [jnp.where(prow >= pcol, wsp_ref[gi], 0.0).astype(BF16) for gi in range(N_GROUPS)]
    lng = lng_ref[...]
    lnb = lnb_ref[...]
    bias = bsp_ref[...]

    def spatial_body(ci, carry):
        r0 = pl.multiple_of(ci * p, p)
        u = jax.nn.gelu(proj_ref[pl.ds(r0, p), 4 * A_WIDTH:4 * A_WIDTH + B_WIDTH])
        vv = jax.nn.gelu(proj_ref[pl.ds(r0, p), 4 * A_WIDTH + B_WIDTH:4 * A_WIDTH + 2 * B_WIDTH])
        mu = jnp.mean(vv, axis=-1, keepdims=True)
        var = jnp.mean(jnp.square(vv - mu), axis=-1, keepdims=True)
        vn = ((vv - mu) * lax.rsqrt(var + EPS) * lng + lnb).astype(BF16)
        for gi in range(N_GROUPS):
            lo, hi = gi * GROUP_DIM, (gi + 1) * GROUP_DIM
            s = _dot(w_tril[gi], vn[:, lo:hi]) + bias[:, lo:hi]
            mix_ref[pl.ds(r0, p), A_WIDTH + lo:A_WIDTH + hi] = (u[:, lo:hi] * s).astype(BF16)
        return carry

    lax.fori_loop(0, MIX_TILE // p, spatial_body, 0)

    o_ref[0] = x + _dot(mix_ref[...], wout_ref[...])


def _mixer(x, gmix, win, lbp, hnorm, lng, lnb, wsp, bsp, wout, *, layer):
    bsz, seq, _ = x.shape
    tile = pl.BlockSpec((1, MIX_TILE, D_MODEL), lambda b, s: (b, s, 0))
    return pl.pallas_call(
        functools.partial(_mixer_kernel, layer=layer),
        grid=(bsz, seq // MIX_TILE),
        in_specs=[tile, _resident((1, D_MODEL)), _resident((D_MODEL, D_IN)), _resident((DEPTH, A_WIDTH)),
                  _resident((1, A_WIDTH)), _resident((1, B_WIDTH)), _resident((1, B_WIDTH)),
                  _resident((N_GROUPS, SPATIAL_CHUNK, SPATIAL_CHUNK)), _resident((SPATIAL_CHUNK, B_WIDTH)),
                  _resident((D_MODEL, D_MODEL))],
        out_specs=tile,
        out_shape=jax.ShapeDtypeStruct(x.shape, F32),
        scratch_shapes=[pltpu.VMEM((N_HEADS, HEAD_DIM, HEAD_DIM), F32),
                        pltpu.VMEM((MIX_TILE, D_IN), F32),
                        pltpu.VMEM((MIX_TILE, D_MODEL), BF16)],
        compiler_params=pltpu.CompilerParams(dimension_semantics=("arbitrary", "arbitrary"),
                                             vmem_limit_bytes=VMEM_LIMIT_BYTES),
        name="mixer",
    )(x, gmix, win, lbp, hnorm, lng, lnb, wsp, bsp, wout)


def kernel(x, norm_ffn1, ffn1_w_gate, ffn1_w_up, ffn1_w_down, norm_mix, w_in, lb_param, hgrn_norm, ln_v_gain,
           ln_v_bias, w_spatial, b_spatial, w_out, norm_ffn2, ffn2_w_gate, ffn2_w_up, ffn2_w_down, norm_final):
    bsz, seq, d = x.shape
    row = lambda t: t.reshape(1, -1).astype(F32)
    lbp = lb_param.astype(F32)
    gf = row(norm_final)
    for l in range(DEPTH):
        x = _ffn(x.reshape(bsz * seq, d), row(norm_ffn1[l]), ffn1_w_gate[l].astype(BF16),
                 ffn1_w_up[l].astype(BF16), ffn1_w_down[l].astype(BF16), gf, final=False).reshape(bsz, seq, d)
        bsp = jnp.repeat(b_spatial[l].astype(F32).T, GROUP_DIM, axis=1)
        x = _mixer(x, row(norm_mix[l]), w_in[l].astype(BF16), lbp, row(hgrn_norm[l]), row(ln_v_gain[l]),
                   row(ln_v_bias[l]), w_spatial[l].astype(F32), bsp, w_out[l].astype(BF16), layer=l)
        x = _ffn(x.reshape(bsz * seq, d), row(norm_ffn2[l]), ffn2_w_gate[l].astype(BF16),
                 ffn2_w_up[l].astype(BF16), ffn2_w_down[l].astype(BF16), gf,
                 final=(l == DEPTH - 1)).reshape(bsz, seq, d)
    return x
```

```python
import functools
import math

import jax
import jax.numpy as jnp
import numpy as np
from jax import lax
from jax.experimental import pallas as pl
from jax.experimental.pallas import tpu as pltpu

D_MODEL = 1024
DEPTH = 2
A_WIDTH = 512
HEAD_DIM = 128
N_HEADS = A_WIDTH // HEAD_DIM
B_WIDTH = 512
N_GROUPS = 4
GROUP_DIM = B_WIDTH // N_GROUPS
D_IN = 4 * A_WIDTH + 2 * B_WIDTH
D_FF = int(math.ceil(8 * D_MODEL / 3 / 128)) * 128
SPATIAL_CHUNK = 128
RECUR_CHUNK = 64
FFN_RES = 0.5
EPS = 1e-6
F_MIN = 1e-20

SUBLANES = 8
LEVELS = (32, 16, 8, 4, 2)
N_DECAY = 2 + len(LEVELS)
FFN_TILE = 512
MIX_TILE = 512
VMEM_LIMIT_BYTES = 56 * 1024 * 1024

F32 = jnp.float32
BF16 = jnp.bfloat16


def _rmsnorm(x, g):
    return x * lax.rsqrt(jnp.mean(x * x, axis=-1, keepdims=True) + EPS) * g


def _dot(a, b):
    return jnp.dot(a, b, preferred_element_type=F32)


def _dot_nt(a, b):
    return lax.dot_general(a, b, (((1,), (1,)), ((), ())), preferred_element_type=F32)


def _dot_tn(a, b):
    return lax.dot_general(a, b, (((0,), (0,)), ((), ())), preferred_element_type=F32)


def _ffn_kernel(x_ref, g_ref, wg_ref, wu_ref, wd_ref, gf_ref, o_ref, *, final):
    x = x_ref[...]
    h = _rmsnorm(x, g_ref[...]).astype(BF16)
    gate = _dot(h, wg_ref[...])
    up = _dot(h, wu_ref[...])
    act = (gate * jax.nn.sigmoid(gate) * up).astype(BF16)
    out = x + FFN_RES * _dot(act, wd_ref[...])
    if final:
        out = _rmsnorm(out, gf_ref[...])
    o_ref[...] = out


def _resident(shape):
    return pl.BlockSpec(shape, lambda *_: (0,) * len(shape), pipeline_mode=pl.Buffered(1))


def _ffn(x2d, g, wg, wu, wd, gf, *, final):
    m = x2d.shape[0]
    tile = pl.BlockSpec((FFN_TILE, D_MODEL), lambda i: (i, 0))
    return pl.pallas_call(
        functools.partial(_ffn_kernel, final=final),
        grid=(m // FFN_TILE,),
        in_specs=[tile, _resident((1, D_MODEL)), _resident((D_MODEL, D_FF)), _resident((D_MODEL, D_FF)),
                  _resident((D_FF, D_MODEL)), _resident((1, D_MODEL))],
        out_specs=tile,
        out_shape=jax.ShapeDtypeStruct(x2d.shape, F32),
        compiler_params=pltpu.CompilerParams(dimension_semantics=("arbitrary",),
                                             vmem_limit_bytes=VMEM_LIMIT_BYTES),
        name="ffn_final" if final else "ffn",
    )(x2d, g, wg, wu, wd, gf)


def _split2(x):
    hi = x.astype(BF16)
    lo = (x - hi.astype(F32)).astype(BF16)
    return hi, lo


def _decay_sum_matrix():
    c = RECUR_CHUNK
    t = np.arange(c)[:, None]
    u = np.arange(c)[None, :]
    blocks = [u <= t, u > t]
    for m in LEVELS:
        same = (u // m) == (t // m)
        odd = (t // m) % 2 == 1
        blocks.append(same & np.where(odd, u <= t, u > t))
    return np.concatenate(blocks, axis=0).astype(np.float32)


def _level_ids():
    c = RECUR_CHUNK
    t = np.arange(c)[:, None]
    s = np.arange(c)[None, :]
    ids = np.full((c, c), -1, np.int32)
    ids[((t // 2) == (s // 2)) & (s <= t)] = 0
    for i, m in enumerate(LEVELS):
        pair = ((t // (2 * m)) == (s // (2 * m))) & ((t // m) % 2 == 1) & ((s // m) % 2 == 0)
        ids[pair] = 1 + i
    return ids


def _rows3(x):
    return x.reshape(RECUR_CHUNK // SUBLANES, SUBLANES, A_WIDTH)


def _odd_rows(m):
    return (lax.broadcasted_iota(jnp.int32, (1, SUBLANES, A_WIDTH), 1) // m) % 2 == 1


def _pair_operand(q, kk, x, m):
    c = RECUR_CHUNK
    if m >= SUBLANES:
        parts = [(q if b % 2 else kk)[b * m:(b + 1) * m] * x[b * m:(b + 1) * m] for b in range(c // m)]
        return jnp.concatenate(parts, axis=0)
    return (jnp.where(_odd_rows(m), _rows3(q), _rows3(kk)) * _rows3(x)).reshape(c, A_WIDTH)


def _hgrn2_operands(q, z, lb, tmat):
    c = RECUR_CHUNK
    e = jnp.exp(-jnp.abs(z))
    inv = 1.0 / (1.0 + e)
    sig_pos = jnp.where(z >= 0, 1.0, e) * inv
    sig_neg = jnp.where(z >= 0, e, 1.0) * inv
    fc = jnp.maximum(lb + (1.0 - lb) * sig_pos, F_MIN)
    lf = jnp.log(fc)
    kk = (1.0 - lb) * sig_neg

    hi, lo = _split2(lf)
    xx = jnp.exp(_dot(tmat, jnp.concatenate([hi, lo], axis=0)))
    xa = xx[0:c]
    odd = _odd_rows(1)
    return dict(
        qa=(q * xa).astype(BF16),
        kd=(kk * xx[c:2 * c]).astype(BF16),
        dec=xa[c - 1:c, :],
        ws=[_pair_operand(q, kk, xx[(2 + i) * c:(3 + i) * c], m).astype(BF16) for i, m in enumerate(LEVELS)],
        q2=jnp.where(odd, _rows3(q * fc), _rows3(q)).reshape(c, A_WIDTH).astype(BF16),
        k2=jnp.where(odd, _rows3(kk / fc), _rows3(kk)).reshape(c, A_WIDTH).astype(BF16),
    )


def _hgrn2_products(ops, vb, states):
    new_states, scores, inter = [], [], []
    for hd in range(N_HEADS):
        sl = slice(hd * HEAD_DIM, (hd + 1) * HEAD_DIM)
        new_states.append(states[hd] * ops["dec"][:, sl] + _dot_tn(vb[:, sl], ops["kd"][:, sl]))
    for hd in range(N_HEADS):
        sl = slice(hd * HEAD_DIM, (hd + 1) * HEAD_DIM)
        scores.append([_dot_nt(ops["q2"][:, sl], ops["k2"][:, sl])] + [_dot_nt(w[:, sl], w[:, sl]) for w in ops["ws"]])
    for hd in range(N_HEADS):
        sl = slice(hd * HEAD_DIM, (hd + 1) * HEAD_DIM)
        inter.append(_dot_nt(ops["qa"][:, sl], states[hd].astype(BF16)))
    return new_states, scores, inter


def _hgrn2_outputs(scores, inter, vb, g, hnorm, lvl):
    outs = []
    for hd in range(N_HEADS):
        sl = slice(hd * HEAD_DIM, (hd + 1) * HEAD_DIM)
        p = jnp.where(lvl == 0, scores[hd][0], 0.0)
        for i in range(len(LEVELS)):
            p = jnp.where(lvl == 1 + i, scores[hd][1 + i], p)
        o = inter[hd] + _dot(p.astype(BF16), vb[:, sl])
        o = o * lax.rsqrt(jnp.mean(o * o, axis=-1, keepdims=True) + EPS) * hnorm[:, sl]
        gh = g[:, sl]
        outs.append((o * (gh * jax.nn.sigmoid(gh))).astype(BF16))
    return jnp.concatenate(outs, axis=1)


def _mixer_kernel(x_ref, gmix_ref, win_ref, lbp_ref, hnorm_ref, lng_ref, lnb_ref, wsp_ref, bsp_ref, wout_ref,
                  tmat_ref, lvl_ref, o_ref, state_ref, proj_ref, mix_ref, *, layer):
    c = RECUR_CHUNK
    n_chunks = MIX_TILE // c

    @pl.when(pl.program_id(1) == 0)
    def _():
        state_ref[...] = jnp.zeros(state_ref.shape, F32)

    x = x_ref[0]
    h = _rmsnorm(x, gmix_ref[...]).astype(BF16)
    proj_ref[...] = _dot(h, win_ref[...])

    lp = lbp_ref[...]
    ex = jnp.exp(lp - jnp.max(lp, axis=0, keepdims=True))
    prob = ex / jnp.sum(ex, axis=0, keepdims=True)
    lbs = jnp.sum(prob[:layer + 1, :], axis=0, keepdims=True) - prob[0:1, :]
    hnorm = hnorm_ref[...]
    tmat = tmat_ref[...]
    lvl = lvl_ref[...]

    def field(ci, k):
        return proj_ref[ci * c:(ci + 1) * c, k * A_WIDTH:(k + 1) * A_WIDTH]

    operands, values, products = {}, {}, {}

    def prepare(ci):
        operands[ci] = _hgrn2_operands(field(ci, 0), field(ci, 1), lbs, tmat)
        values[ci] = field(ci, 2).astype(BF16)

    def finish(ci):
        scores, inter = products.pop(ci)
        mix_ref[ci * c:(ci + 1) * c, 0:A_WIDTH] = _hgrn2_outputs(scores, inter, values.pop(ci), field(ci, 3),
                                                                  hnorm, lvl)

    states = [state_ref[hd] for hd in range(N_HEADS)]
    prepare(0)
    prepare(1)
    for ci in range(n_chunks):
        states, scores, inter = _hgrn2_products(operands.pop(ci), values[ci], states)
        products[ci] = (scores, inter)
        if ci >= 1:
            finish(ci - 1)
        if ci + 2 < n_chunks:
            prepare(ci + 2)
    finish(n_chunks - 1)
    for hd in range(N_HEADS):
        state_ref[hd] = states[hd]

    p = SPATIAL_CHUNK
    prow = lax.broadcasted_iota(jnp.int32, (p, p), 0)
    pcol = lax.broadcasted_iota(jnp.int32, (p, p), 1)
    w_tril = [jnp.where(prow >= pcol, wsp_ref[gi], 0.0).astype(BF16) for gi in range(N_GROUPS)]
    lng = lng_ref[...]
    lnb = lnb_ref[...]
    bias = bsp_ref[...]

    def spatial_body(ci, carry):
        r0 = pl.multiple_of(ci * p, p)
        u = jax.nn.gelu(proj_ref[pl.ds(r0, p), 4 * A_WIDTH:4 * A_WIDTH + B_WIDTH])
        vv = jax.nn.gelu(proj_ref[pl.ds(r0, p), 4 * A_WIDTH + B_WIDTH:4 * A_WIDTH + 2 * B_WIDTH])
        mu = jnp.mean(vv, axis=-1, keepdims=True)
        var = jnp.mean(jnp.square(vv - mu), axis=-1, keepdims=True)
        vn = ((vv - mu) * lax.rsqrt(var + EPS) * lng + lnb).astype(BF16)
        for gi in range(N_GROUPS):
            lo, hi = gi * GROUP_DIM, (gi + 1) * GROUP_DIM
            s = _dot(w_tril[gi], vn[:, lo:hi]) + bias[:, lo:hi]
            mix_ref[pl.ds(r0, p), A_WIDTH + lo:A_WIDTH + hi] = (u[:, lo:hi] * s).astype(BF16)
        return carry

    lax.fori_loop(0, MIX_TILE // p, spatial_body, 0)

    o_ref[0] = x + _dot(mix_ref[...], wout_ref[...])


def _mixer(x, gmix, win, lbp, hnorm, lng, lnb, wsp, bsp, wout, *, layer):
    bsz, seq, _ = x.shape
    c = RECUR_CHUNK
    tile = pl.BlockSpec((1, MIX_TILE, D_MODEL), lambda b, s: (b, s, 0))
    tmat = _decay_sum_matrix()
    tmat2 = jnp.asarray(np.concatenate([tmat, tmat], axis=1), BF16)
    lvl = jnp.asarray(_level_ids())
    return pl.pallas_call(
        functools.partial(_mixer_kernel, layer=layer),
        grid=(bsz, seq // MIX_TILE),
        in_specs=[tile, _resident((1, D_MODEL)), _resident((D_MODEL, D_IN)), _resident((DEPTH, A_WIDTH)),
                  _resident((1, A_WIDTH)), _resident((1, B_WIDTH)), _resident((1, B_WIDTH)),
                  _resident((N_GROUPS, SPATIAL_CHUNK, SPATIAL_CHUNK)), _resident((SPATIAL_CHUNK, B_WIDTH)),
                  _resident((D_MODEL, D_MODEL)), _resident((N_DECAY * c, 2 * c)), _resident((c, c))],
        out_specs=tile,
        out_shape=jax.ShapeDtypeStruct(x.shape, F32),
        scratch_shapes=[pltpu.VMEM((N_HEADS, HEAD_DIM, HEAD_DIM), F32),
                        pltpu.VMEM((MIX_TILE, D_IN), F32),
                        pltpu.VMEM((MIX_TILE, D_MODEL), BF16)],
        compiler_params=pltpu.CompilerParams(dimension_semantics=("arbitrary", "arbitrary"),
                                             vmem_limit_bytes=VMEM_LIMIT_BYTES),
        name="mixer",
    )(x, gmix, win, lbp, hnorm, lng, lnb, wsp, bsp, wout, tmat2, lvl)


def kernel(x, norm_ffn1, ffn1_w_gate, ffn1_w_up, ffn1_w_down, norm_mix, w_in, lb_param, hgrn_norm, ln_v_gain,
           ln_v_bias, w_spatial, b_spatial, w_out, norm_ffn2, ffn2_w_gate, ffn2_w_up, ffn2_w_down, norm_final):
    bsz, seq, d = x.shape
    row = lambda t: t.reshape(1, -1).astype(F32)
    lbp = lb_param.astype(F32)
    gf = row(norm_final)
    for l in range(DEPTH):
        x = _ffn(x.reshape(bsz * seq, d), row(norm_ffn1[l]), ffn1_w_gate[l].astype(BF16),
                 ffn1_w_up[l].astype(BF16), ffn1_w_down[l].astype(BF16), gf, final=False).reshape(bsz, seq, d)
        bsp = jnp.repeat(b_spatial[l].astype(F32).T, GROUP_DIM, axis=1)
        x = _mixer(x, row(norm_mix[l]), w_in[l].astype(BF16), lbp, row(hgrn_norm[l]), row(ln_v_gain[l]),
                   row(ln_v_bias[l]), w_spatial[l].astype(F32), bsp, w_out[l].astype(BF16), layer=l)
        x = _ffn(x.reshape(bsz * seq, d), row(norm_ffn2[l]), ffn2_w_gate[l].astype(BF16),
                 ffn2_w_up[l].astype(BF16), ffn2_w_down[l].astype(BF16), gf,
                 final=(l == DEPTH - 1)).reshape(bsz, seq, d)
    return x
```

```python
import functools
import math

import jax
import jax.numpy as jnp
import numpy as np
from jax import lax
from jax.experimental import pallas as pl
from jax.experimental.pallas import tpu as pltpu

D_MODEL = 1024
DEPTH = 2
A_WIDTH = 512
HEAD_DIM = 128
N_HEADS = A_WIDTH // HEAD_DIM
B_WIDTH = 512
N_GROUPS = 4
GROUP_DIM = B_WIDTH // N_GROUPS
D_IN = 4 * A_WIDTH + 2 * B_WIDTH
D_FF = int(math.ceil(8 * D_MODEL / 3 / 128)) * 128
SPATIAL_CHUNK = 128
RECUR_CHUNK = 64
FFN_RES = 0.5
EPS = 1e-6
F_MIN = 1e-20

SUBLANES = 8
LEVELS = (32, 16, 8, 4, 2)
N_DECAY = 2 + len(LEVELS)
FFN_TILE = 512
MIX_TILE = 512
VMEM_LIMIT_BYTES = 56 * 1024 * 1024

F32 = jnp.float32
BF16 = jnp.bfloat16


def _rmsnorm(x, g):
    return x * lax.rsqrt(jnp.mean(x * x, axis=-1, keepdims=True) + EPS) * g


def _dot(a, b):
    return jnp.dot(a, b, preferred_element_type=F32)


def _dot_nt(a, b):
    return lax.dot_general(a, b, (((1,), (1,)), ((), ())), preferred_element_type=F32)


def _dot_tn(a, b):
    return lax.dot_general(a, b, (((0,), (0,)), ((), ())), preferred_element_type=F32)


def _ffn_kernel(x_ref, g_ref, wg_ref, wu_ref, wd_ref, gf_ref, o_ref, *, final):
    x = x_ref[...]
    h = _rmsnorm(x, g_ref[...]).astype(BF16)
    gate = _dot(h, wg_ref[...])
    up = _dot(h, wu_ref[...])
    act = (gate * jax.nn.sigmoid(gate) * up).astype(BF16)
    out = x + FFN_RES * _dot(act, wd_ref[...])
    if final:
        out = _rmsnorm(out, gf_ref[...])
    o_ref[...] = out


def _resident(shape):
    return pl.BlockSpec(shape, lambda *_: (0,) * len(shape), pipeline_mode=pl.Buffered(1))


def _ffn(x2d, g, wg, wu, wd, gf, *, final):
    m = x2d.shape[0]
    tile = pl.BlockSpec((FFN_TILE, D_MODEL), lambda i: (i, 0))
    return pl.pallas_call(
        functools.partial(_ffn_kernel, final=final),
        grid=(m // FFN_TILE,),
        in_specs=[tile, _resident((1, D_MODEL)), _resident((D_MODEL, D_FF)), _resident((D_MODEL, D_FF)),
                  _resident((D_FF, D_MODEL)), _resident((1, D_MODEL))],
        out_specs=tile,
        out_shape=jax.ShapeDtypeStruct(x2d.shape, F32),
        compiler_params=pltpu.CompilerParams(dimension_semantics=("arbitrary",),
                                             vmem_limit_bytes=VMEM_LIMIT_BYTES),
        name="ffn_final" if final else "ffn",
    )(x2d, g, wg, wu, wd, gf)


def _split2(x):
    hi = x.astype(BF16)
    lo = (x - hi.astype(F32)).astype(BF16)
    return hi, lo


def _decay_sum_matrix():
    c = RECUR_CHUNK
    t = np.arange(c)[:, None]
    u = np.arange(c)[None, :]
    blocks = [u <= t, u > t]
    for m in LEVELS:
        same = (u // m) == (t // m)
        odd = (t // m) % 2 == 1
        blocks.append(same & np.where(odd, u <= t, u > t))
    return np.concatenate(blocks, axis=0).astype(np.float32)


def _level_ids():
    c = RECUR_CHUNK
    t = np.arange(c)[:, None]
    s = np.arange(c)[None, :]
    ids = np.full((c, c), -1, np.int32)
    ids[((t // 2) == (s // 2)) & (s <= t)] = 0
    for i, m in enumerate(LEVELS):
        pair = ((t // (2 * m)) == (s // (2 * m))) & ((t // m) % 2 == 1) & ((s // m) % 2 == 0)
        ids[pair] = 1 + i
    return ids


def _rows3(x):
    return x.reshape(RECUR_CHUNK // SUBLANES, SUBLANES, A_WIDTH)


def _odd_rows(m):
    return (lax.broadcasted_iota(jnp.int32, (1, SUBLANES, A_WIDTH), 1) // m) % 2 == 1


def _pair_operand(q, kk, x, m):
    c = RECUR_CHUNK
    if m >= SUBLANES:
        parts = [(q if b % 2 else kk)[b * m:(b + 1) * m] * x[b * m:(b + 1) * m] for b in range(c // m)]
        return jnp.concatenate(parts, axis=0)
    return (jnp.where(_odd_rows(m), _rows3(q), _rows3(kk)) * _rows3(x)).reshape(c, A_WIDTH)


def _hgrn2_operands(q, z, lb, tmat):
    c = RECUR_CHUNK
    e = jnp.exp(-jnp.abs(z))
    inv = 1.0 / (1.0 + e)
    sig_pos = jnp.where(z >= 0, 1.0, e) * inv
    sig_neg = jnp.where(z >= 0, e, 1.0) * inv
    fc = jnp.maximum(lb + (1.0 - lb) * sig_pos, F_MIN)
    lf = jnp.log(fc)
    kk = (1.0 - lb) * sig_neg

    hi, lo = _split2(lf)
    xx = jnp.exp(_dot(tmat, jnp.concatenate([hi, lo], axis=0)))
    xa = xx[0:c]
    odd = _odd_rows(1)
    return dict(
        qa=(q * xa).astype(BF16),
        kd=(kk * xx[c:2 * c]).astype(BF16),
        dec=xa[c - 1:c, :],
        ws=[_pair_operand(q, kk, xx[(2 + i) * c:(3 + i) * c], m).astype(BF16) for i, m in enumerate(LEVELS)],
        q2=jnp.where(odd, _rows3(q * fc), _rows3(q)).reshape(c, A_WIDTH).astype(BF16),
        k2=jnp.where(odd, _rows3(kk / fc), _rows3(kk)).reshape(c, A_WIDTH).astype(BF16),
    )


def _hgrn2_products(ops, vb, states):
    new_states, scores, inter = [], [], []
    for hd in range(N_HEADS):
        sl = slice(hd * HEAD_DIM, (hd + 1) * HEAD_DIM)
        new_states.append(states[hd] * ops["dec"][:, sl] + _dot_tn(vb[:, sl], ops["kd"][:, sl]))
    for hd in range(N_HEADS):
        sl = slice(hd * HEAD_DIM, (hd + 1) * HEAD_DIM)
        scores.append([_dot_nt(ops["q2"][:, sl], ops["k2"][:, sl])] + [_dot_nt(w[:, sl], w[:, sl]) for w in ops["ws"]])
    for hd in range(N_HEADS):
        sl = slice(hd * HEAD_DIM, (hd + 1) * HEAD_DIM)
        inter.append(_dot_nt(ops["qa"][:, sl], states[hd].astype(BF16)))
    return new_states, scores, inter


def _hgrn2_outputs(scores, inter, vb, g, hnorm, lvl):
    outs = []
    for hd in range(N_HEADS):
        sl = slice(hd * HEAD_DIM, (hd + 1) * HEAD_DIM)
        p = jnp.where(lvl == 0, scores[hd][0], 0.0)
        for i in range(len(LEVELS)):
            p = jnp.where(lvl == 1 + i, scores[hd][1 + i], p)
        o = inter[hd] + _dot(p.astype(BF16), vb[:, sl])
        o = o * lax.rsqrt(jnp.mean(o * o, axis=-1, keepdims=True) + EPS) * hnorm[:, sl]
        gh = g[:, sl]
        outs.append((o * (gh * jax.nn.sigmoid(gh))).astype(BF16))
    return jnp.concatenate(outs, axis=1)


def _mixer_kernel(x_ref, gmix_ref, win_ref, lbp_ref, hnorm_ref, lng_ref, lnb_ref, wsp_ref, bsp_ref, wout_ref,
                  tmat_ref, lvl_ref, o_ref, state_ref, mix_ref, *, layer):
    c = RECUR_CHUNK
    n_chunks = MIX_TILE // c

    @pl.when(pl.program_id(1) == 0)
    def _():
        state_ref[...] = jnp.zeros(state_ref.shape, F32)

    x = x_ref[0]
    h = _rmsnorm(x, gmix_ref[...]).astype(BF16)
    proj = {}
    for k in (1, 0, 2, 4, 5, 3):
        proj[k] = _dot(h, win_ref[:, k * A_WIDTH:(k + 1) * A_WIDTH])

    lp = lbp_ref[...]
    ex = jnp.exp(lp - jnp.max(lp, axis=0, keepdims=True))
    prob = ex / jnp.sum(ex, axis=0, keepdims=True)
    lbs = jnp.sum(prob[:layer + 1, :], axis=0, keepdims=True) - prob[0:1, :]
    hnorm = hnorm_ref[...]
    tmat = tmat_ref[...]
    lvl = lvl_ref[...]

    p = SPATIAL_CHUNK
    prow = lax.broadcasted_iota(jnp.int32, (p, p), 0)
    pcol = lax.broadcasted_iota(jnp.int32, (p, p), 1)
    w_tril = [jnp.where(prow >= pcol, wsp_ref[gi], 0.0).astype(BF16) for gi in range(N_GROUPS)]
    lng = lng_ref[...]
    lnb = lnb_ref[...]
    bias = bsp_ref[...]

    def spatial(si):
        rows = slice(si * p, (si + 1) * p)
        u = jax.nn.gelu(proj[4][rows])
        vv = jax.nn.gelu(proj[5][rows])
        mu = jnp.mean(vv, axis=-1, keepdims=True)
        var = jnp.mean(jnp.square(vv - mu), axis=-1, keepdims=True)
        vn = ((vv - mu) * lax.rsqrt(var + EPS) * lng + lnb).astype(BF16)
        for gi in range(N_GROUPS):
            lo, hi = gi * GROUP_DIM, (gi + 1) * GROUP_DIM
            s = _dot(w_tril[gi], vn[:, lo:hi]) + bias[:, lo:hi]
            mix_ref[rows, A_WIDTH + lo:A_WIDTH + hi] = (u[:, lo:hi] * s).astype(BF16)

    def field(ci, k):
        return proj[k][ci * c:(ci + 1) * c]

    operands, values, products = {}, {}, {}

    def prepare(ci):
        operands[ci] = _hgrn2_operands(field(ci, 0), field(ci, 1), lbs, tmat)
        values[ci] = field(ci, 2).astype(BF16)

    def finish(ci):
        scores, inter = products.pop(ci)
        mix_ref[ci * c:(ci + 1) * c, 0:A_WIDTH] = _hgrn2_outputs(scores, inter, values.pop(ci), field(ci, 3),
                                                                  hnorm, lvl)

    states = [state_ref[hd] for hd in range(N_HEADS)]
    prepare(0)
    prepare(1)
    for ci in range(n_chunks):
        states, scores, inter = _hgrn2_products(operands.pop(ci), values[ci], states)
        products[ci] = (scores, inter)
        if ci >= 1:
            finish(ci - 1)
        if ci + 2 < n_chunks:
            prepare(ci + 2)
        if ci % (p // c) == 0:
            spatial(ci // (p // c))
    finish(n_chunks - 1)
    for hd in range(N_HEADS):
        state_ref[hd] = states[hd]

    o_ref[0] = x + _dot(mix_ref[...], wout_ref[...])


def _mixer(x, gmix, win, lbp, hnorm, lng, lnb, wsp, bsp, wout, *, layer):
    bsz, seq, _ = x.shape
    c = RECUR_CHUNK
    tile = pl.BlockSpec((1, MIX_TILE, D_MODEL), lambda b, s: (b, s, 0))
    tmat = _decay_sum_matrix()
    tmat2 = jnp.asarray(np.concatenate([tmat, tmat], axis=1), BF16)
    lvl = jnp.asarray(_level_ids())
    return pl.pallas_call(
        functools.partial(_mixer_kernel, layer=layer),
        grid=(bsz, seq // MIX_TILE),
        in_specs=[tile, _resident((1, D_MODEL)), _resident((D_MODEL, D_IN)), _resident((DEPTH, A_WIDTH)),
                  _resident((1, A_WIDTH)), _resident((1, B_WIDTH)), _resident((1, B_WIDTH)),
                  _resident((N_GROUPS, SPATIAL_CHUNK, SPATIAL_CHUNK)), _resident((SPATIAL_CHUNK, B_WIDTH)),
                  _resident((D_MODEL, D_MODEL)), _resident((N_DECAY * c, 2 * c)), _resident((c, c))],
        out_specs=tile,
        out_shape=jax.ShapeDtypeStruct(x.shape, F32),
        scratch_shapes=[pltpu.VMEM((N_HEADS, HEAD_DIM, HEAD_DIM), F32),
                        pltpu.VMEM((MIX_TILE, D_MODEL), BF16)],
        compiler_params=pltpu.CompilerParams(dimension_semantics=("arbitrary", "arbitrary"),
                                             vmem_limit_bytes=VMEM_LIMIT_BYTES),
        name="mixer",
    )(x, gmix, win, lbp, hnorm, lng, lnb, wsp, bsp, wout, tmat2, lvl)


def kernel(x, norm_ffn1, ffn1_w_gate, ffn1_w_up, ffn1_w_down, norm_mix, w_in, lb_param, hgrn_norm, ln_v_gain,
           ln_v_bias, w_spatial, b_spatial, w_out, norm_ffn2, ffn2_w_gate, ffn2_w_up, ffn2_w_down, norm_final):
    bsz, seq, d = x.shape
    row = lambda t: t.reshape(1, -1).astype(F32)
    lbp = lb_param.astype(F32)
    gf = row(norm_final)
    for l in range(DEPTH):
        x = _ffn(x.reshape(bsz * seq, d), row(norm_ffn1[l]), ffn1_w_gate[l].astype(BF16),
                 ffn1_w_up[l].astype(BF16), ffn1_w_down[l].astype(BF16), gf, final=False).reshape(bsz, seq, d)
        bsp = jnp.repeat(b_spatial[l].astype(F32).T, GROUP_DIM, axis=1)
        x = _mixer(x, row(norm_mix[l]), w_in[l].astype(BF16), lbp, row(hgrn_norm[l]), row(ln_v_gain[l]),
                   row(ln_v_bias[l]), w_spatial[l].astype(F32), bsp, w_out[l].astype(BF16), layer=l)
        x = _ffn(x.reshape(bsz * seq, d), row(norm_ffn2[l]), ffn2_w_gate[l].astype(BF16),
                 ffn2_w_up[l].astype(BF16), ffn2_w_down[l].astype(BF16), gf,
                 final=(l == DEPTH - 1)).reshape(bsz, seq, d)
    return x
```

```python
import functools
import math

import jax
import jax.numpy as jnp
import numpy as np
from jax import lax
from jax.experimental import pallas as pl
from jax.experimental.pallas import tpu as pltpu

D_MODEL = 1024
DEPTH = 2
A_WIDTH = 512
HEAD_DIM = 128
N_HEADS = A_WIDTH // HEAD_DIM
B_WIDTH = 512
N_GROUPS = 4
GROUP_DIM = B_WIDTH // N_GROUPS
D_IN = 4 * A_WIDTH + 2 * B_WIDTH
D_FF = int(math.ceil(8 * D_MODEL / 3 / 128)) * 128
SPATIAL_CHUNK = 128
RECUR_CHUNK = 64
FFN_RES = 0.5
EPS = 1e-6
F_MIN = 1e-20

SUBLANES = 8
LEVELS = (32, 16, 8, 4, 2)
N_SPLIT = 3
FFN_TILE = 512
MIX_TILE = 512
VMEM_LIMIT_BYTES = 56 * 1024 * 1024

F32 = jnp.float32
BF16 = jnp.bfloat16


def _rmsnorm(x, g):
    return x * lax.rsqrt(jnp.mean(x * x, axis=-1, keepdims=True) + EPS) * g


def _dot(a, b):
    return jnp.dot(a, b, preferred_element_type=F32)


def _dot_nt(a, b):
    return lax.dot_general(a, b, (((1,), (1,)), ((), ())), preferred_element_type=F32)


def _dot_tn(a, b):
    return lax.dot_general(a, b, (((0,), (0,)), ((), ())), preferred_element_type=F32)


def _ffn_kernel(x_ref, g_ref, wg_ref, wu_ref, wd_ref, gf_ref, o_ref, *, final):
    x = x_ref[...]
    h = _rmsnorm(x, g_ref[...]).astype(BF16)
    gate = _dot(h, wg_ref[...])
    up = _dot(h, wu_ref[...])
    act = (gate * jax.nn.sigmoid(gate) * up).astype(BF16)
    out = x + FFN_RES * _dot(act, wd_ref[...])
    if final:
        out = _rmsnorm(out, gf_ref[...])
    o_ref[...] = out


def _resident(shape):
    return pl.BlockSpec(shape, lambda *_: (0,) * len(shape), pipeline_mode=pl.Buffered(1))


def _ffn(x2d, g, wg, wu, wd, gf, *, final):
    m = x2d.shape[0]
    tile = pl.BlockSpec((FFN_TILE, D_MODEL), lambda i: (i, 0))
    return pl.pallas_call(
        functools.partial(_ffn_kernel, final=final),
        grid=(m // FFN_TILE,),
        in_specs=[tile, _resident((1, D_MODEL)), _resident((D_MODEL, D_FF)), _resident((D_MODEL, D_FF)),
                  _resident((D_FF, D_MODEL)), _resident((1, D_MODEL))],
        out_specs=tile,
        out_shape=jax.ShapeDtypeStruct(x2d.shape, F32),
        compiler_params=pltpu.CompilerParams(dimension_semantics=("arbitrary",),
                                             vmem_limit_bytes=VMEM_LIMIT_BYTES),
        name="ffn_final" if final else "ffn",
    )(x2d, g, wg, wu, wd, gf)


def _split3(x):
    hi = x.astype(BF16)
    r1 = x - hi.astype(F32)
    mid = r1.astype(BF16)
    lo = (r1 - mid.astype(F32)).astype(BF16)
    return jnp.concatenate([hi, mid, lo], axis=0)


def _prefix_sum_matrix():
    c = RECUR_CHUNK
    tril = (np.arange(c)[None, :] <= np.arange(c)[:, None]).astype(np.float32)
    return np.concatenate([tril] * N_SPLIT, axis=1)


def _level_ids():
    c = RECUR_CHUNK
    t = np.arange(c)[:, None]
    s = np.arange(c)[None, :]
    ids = np.full((c, c), -1, np.int32)
    ids[((t // 2) == (s // 2)) & (s <= t)] = 0
    for i, m in enumerate(LEVELS):
        pair = ((t // (2 * m)) == (s // (2 * m))) & ((t // m) % 2 == 1) & ((s // m) % 2 == 0)
        ids[pair] = 1 + i
    return np.concatenate([ids, ids], axis=1)


def _rows3(x):
    return x.reshape(RECUR_CHUNK // SUBLANES, SUBLANES, A_WIDTH)


def _odd_rows(m):
    return (lax.broadcasted_iota(jnp.int32, (1, SUBLANES, A_WIDTH), 1) // m) % 2 == 1


def _pair_operand(q, kk, x, m):
    c = RECUR_CHUNK
    if m >= SUBLANES:
        parts = [(q if b % 2 else kk)[b * m:(b + 1) * m] * x[b * m:(b + 1) * m] for b in range(c // m)]
        return jnp.concatenate(parts, axis=0)
    return (jnp.where(_odd_rows(m), _rows3(q), _rows3(kk)) * _rows3(x)).reshape(c, A_WIDTH)


def _level_decay(a, m):
    c = RECUR_CHUNK
    if m >= SUBLANES:
        parts = []
        for j in range(c // (2 * m)):
            ref = a[(2 * j + 1) * m - 1:(2 * j + 1) * m]
            parts += [ref - a[2 * j * m:(2 * j + 1) * m], a[(2 * j + 1) * m:(2 * j + 2) * m] - ref]
        return jnp.exp2(jnp.concatenate(parts, axis=0))
    a3 = _rows3(a)
    row = lax.broadcasted_iota(jnp.int32, (1, SUBLANES, A_WIDTH), 1)
    ref = a3[:, m - 1:m, :]
    for j in range(1, SUBLANES // (2 * m)):
        ref = jnp.where(row >= 2 * j * m, a3[:, (2 * j + 1) * m - 1:(2 * j + 1) * m, :], ref)
    sign = jnp.where(_odd_rows(m), 1.0, -1.0)
    return jnp.exp2((a3 - ref) * sign).reshape(c, A_WIDTH)


def _hgrn2_operands(q, z, lb, tmat):
    c = RECUR_CHUNK
    e = jnp.exp(-jnp.abs(z))
    inv = 1.0 / (1.0 + e)
    einv = e * inv
    sig_pos = jnp.where(z >= 0, inv, einv)
    sig_neg = jnp.where(z >= 0, einv, inv)
    fc = jnp.maximum(lb + (1.0 - lb) * sig_pos, F_MIN)
    kk = (1.0 - lb) * sig_neg

    a = _dot(tmat, _split3(jnp.log2(fc)))
    xa = jnp.exp2(a)
    odd = _odd_rows(1)
    return dict(
        qa=(q * xa).astype(BF16),
        kd=(kk * jnp.exp2(a[c - 1:c] - a)).astype(BF16),
        dec=xa[c - 1:c, :],
        ws=[_pair_operand(q, kk, _level_decay(a, m), m).astype(BF16) for m in LEVELS],
        q2=jnp.where(odd, _rows3(q * fc), _rows3(q)).reshape(c, A_WIDTH).astype(BF16),
        k2=jnp.where(odd, _rows3(kk / fc), _rows3(kk)).reshape(c, A_WIDTH).astype(BF16),
    )


def _head_pair(x, pi):
    return x[:, 2 * pi * HEAD_DIM:2 * (pi + 1) * HEAD_DIM]


def _block_diag(x2):
    zero = jnp.zeros((x2.shape[0], HEAD_DIM), x2.dtype)
    return jnp.concatenate([jnp.concatenate([x2[:, :HEAD_DIM], zero], axis=1),
                            jnp.concatenate([zero, x2[:, HEAD_DIM:]], axis=1)], axis=0)


def _hgrn2_products(ops, vb, states):
    new_states, scores, inter = [], [], []
    for hd in range(N_HEADS):
        sl = slice(hd * HEAD_DIM, (hd + 1) * HEAD_DIM)
        new_states.append(states[hd] * ops["dec"][:, sl] + _dot_tn(vb[:, sl], ops["kd"][:, sl]))
    for pi in range(N_HEADS // 2):
        pair_scores = [_dot_nt(_head_pair(ops["q2"], pi), _block_diag(_head_pair(ops["k2"], pi)))]
        for w in ops["ws"]:
            pair_scores.append(_dot_nt(_head_pair(w, pi), _block_diag(_head_pair(w, pi))))
        scores.append(pair_scores)
    for hd in range(N_HEADS):
        sl = slice(hd * HEAD_DIM, (hd + 1) * HEAD_DIM)
        inter.append(_dot(ops["qa"][:, sl], states[hd].T.astype(BF16)))
    return new_states, scores, inter


def _hgrn2_outputs(scores, inter, vb, g, hnorm, masks):
    outs = []
    for pi in range(N_HEADS // 2):
        p = jnp.where(masks[0], scores[pi][0], 0.0)
        for i in range(len(LEVELS)):
            p = jnp.where(masks[1 + i], scores[pi][1 + i], p)
        o2 = _dot(p.astype(BF16), _block_diag(_head_pair(vb, pi)))
        for j in range(2):
            hd = 2 * pi + j
            sl = slice(hd * HEAD_DIM, (hd + 1) * HEAD_DIM)
            o = inter[hd] + o2[:, j * HEAD_DIM:(j + 1) * HEAD_DIM]
            o = o * lax.rsqrt(jnp.mean(o * o, axis=-1, keepdims=True) + EPS) * hnorm[:, sl]
            gh = g[:, sl]
            outs.append((o * (gh * jax.nn.sigmoid(gh))).astype(BF16))
    return jnp.concatenate(outs, axis=1)


def _gelu_tanh(x):
    c1 = math.sqrt(2.0 / math.pi)
    return x * (0.5 + 0.5 * jnp.tanh(x * (c1 + (c1 * 0.044715) * (x * x))))


def _mixer_kernel(x_ref, gmix_ref, win_ref, lbp_ref, hnorm_ref, lng_ref, lnb_ref, wsp_ref, bsp_ref, wout_ref,
                  tmat_ref, lvl_ref, o_ref, state_ref, mix_ref, *, layer):
    c = RECUR_CHUNK
    n_chunks = MIX_TILE // c

    @pl.when(pl.program_id(1) == 0)
    def _():
        state_ref[...] = jnp.zeros(state_ref.shape, F32)

    x = x_ref[0]
    h = _rmsnorm(x, gmix_ref[...]).astype(BF16)
    proj = {}
    for k in (1, 0, 2, 4, 5, 3):
        proj[k] = _dot(h, win_ref[:, k * A_WIDTH:(k + 1) * A_WIDTH])

    lp = lbp_ref[...]
    ex = jnp.exp(lp - jnp.max(lp, axis=0, keepdims=True))
    prob = ex / jnp.sum(ex, axis=0, keepdims=True)
    lbs = jnp.sum(prob[:layer + 1, :], axis=0, keepdims=True) - prob[0:1, :]
    hnorm = hnorm_ref[...]
    tmat = tmat_ref[...]
    lvl = lvl_ref[...]
    masks = [lvl == i for i in range(1 + len(LEVELS))]

    p = SPATIAL_CHUNK
    prow = lax.broadcasted_iota(jnp.int32, (p, p), 0)
    pcol = lax.broadcasted_iota(jnp.int32, (p, p), 1)
    w_tril = [jnp.where(prow >= pcol, wsp_ref[gi], 0.0).astype(BF16) for gi in range(N_GROUPS)]
    lng = lng_ref[...]
    lnb = lnb_ref[...]
    bias = bsp_ref[...]

    def spatial(si):
        rows = slice(si * p, (si + 1) * p)
        u = _gelu_tanh(proj[4][rows])
        vv = _gelu_tanh(proj[5][rows])
        mu = jnp.mean(vv, axis=-1, keepdims=True)
        var = jnp.mean(jnp.square(vv - mu), axis=-1, keepdims=True)
        vn = ((vv - mu) * lax.rsqrt(var + EPS) * lng + lnb).astype(BF16)
        for gi in range(N_GROUPS):
            lo, hi = gi * GROUP_DIM, (gi + 1) * GROUP_DIM
            s = _dot(w_tril[gi], vn[:, lo:hi]) + bias[:, lo:hi]
            mix_ref[rows, A_WIDTH + lo:A_WIDTH + hi] = (u[:, lo:hi] * s).astype(BF16)

    def field(ci, k):
        return proj[k][ci * c:(ci + 1) * c]

    operands, values, products = {}, {}, {}

    def prepare(ci):
        operands[ci] = _hgrn2_operands(field(ci, 0), field(ci, 1), lbs, tmat)
        values[ci] = field(ci, 2).astype(BF16)

    def finish(ci):
        scores, inter = products.pop(ci)
        mix_ref[ci * c:(ci + 1) * c, 0:A_WIDTH] = _hgrn2_outputs(scores, inter, values.pop(ci), field(ci, 3),
                                                                  hnorm, masks)

    states = [state_ref[hd] for hd in range(N_HEADS)]
    prepare(0)
    prepare(1)
    for ci in range(n_chunks):
        states, scores, inter = _hgrn2_products(operands.pop(ci), values[ci], states)
        products[ci] = (scores, inter)
        if ci >= 1:
            finish(ci - 1)
        if ci + 2 < n_chunks:
            prepare(ci + 2)
        if ci % (p // c) == 0:
            spatial(ci // (p // c))
    finish(n_chunks - 1)
    for hd in range(N_HEADS):
        state_ref[hd] = states[hd]

    o_ref[0] = x + _dot(mix_ref[...], wout_ref[...])


def _mixer(x, gmix, win, lbp, hnorm, lng, lnb, wsp, bsp, wout, *, layer):
    bsz, seq, _ = x.shape
    c = RECUR_CHUNK
    tile = pl.BlockSpec((1, MIX_TILE, D_MODEL), lambda b, s: (b, s, 0))
    tmat = jnp.asarray(_prefix_sum_matrix(), BF16)
    lvl = jnp.asarray(_level_ids())
    return pl.pallas_call(
        functools.partial(_mixer_kernel, layer=layer),
        grid=(bsz, seq // MIX_TILE),
        in_specs=[tile, _resident((1, D_MODEL)), _resident((D_MODEL, D_IN)), _resident((DEPTH, A_WIDTH)),
                  _resident((1, A_WIDTH)), _resident((1, B_WIDTH)), _resident((1, B_WIDTH)),
                  _resident((N_GROUPS, SPATIAL_CHUNK, SPATIAL_CHUNK)), _resident((SPATIAL_CHUNK, B_WIDTH)),
                  _resident((D_MODEL, D_MODEL)), _resident((c, N_SPLIT * c)), _resident((c, 2 * c))],
        out_specs=tile,
        out_shape=jax.ShapeDtypeStruct(x.shape, F32),
        scratch_shapes=[pltpu.VMEM((N_HEADS, HEAD_DIM, HEAD_DIM), F32),
                        pltpu.VMEM((MIX_TILE, D_MODEL), BF16)],
        compiler_params=pltpu.CompilerParams(dimension_semantics=("arbitrary", "arbitrary"),
                                             vmem_limit_bytes=VMEM_LIMIT_BYTES),
        name="mixer",
    )(x, gmix, win, lbp, hnorm, lng, lnb, wsp, bsp, wout, tmat, lvl)


def kernel(x, norm_ffn1, ffn1_w_gate, ffn1_w_up, ffn1_w_down, norm_mix, w_in, lb_param, hgrn_norm, ln_v_gain,
           ln_v_bias, w_spatial, b_spatial, w_out, norm_ffn2, ffn2_w_gate, ffn2_w_up, ffn2_w_down, norm_final):
    bsz, seq, d = x.shape
    row = lambda t: t.reshape(1, -1).astype(F32)
    lbp = lb_param.astype(F32)
    gf = row(norm_final)
    for l in range(DEPTH):
        x = _ffn(x.reshape(bsz * seq, d), row(norm_ffn1[l]), ffn1_w_gate[l].astype(BF16),
                 ffn1_w_up[l].astype(BF16), ffn1_w_down[l].astype(BF16), gf, final=False).reshape(bsz, seq, d)
        bsp = jnp.repeat(b_spatial[l].astype(F32).T, GROUP_DIM, axis=1)
        x = _mixer(x, row(norm_mix[l]), w_in[l].astype(BF16), lbp, row(hgrn_norm[l]), row(ln_v_gain[l]),
                   row(ln_v_bias[l]), w_spatial[l].astype(F32), bsp, w_out[l].astype(BF16), layer=l)
        x = _ffn(x.reshape(bsz * seq, d), row(norm_ffn2[l]), ffn2_w_gate[l].astype(BF16),
                 ffn2_w_up[l].astype(BF16), ffn2_w_down[l].astype(BF16), gf,
                 final=(l == DEPTH - 1)).reshape(bsz, seq, d)
    return x
```

```python
import functools
import math

import jax
import jax.numpy as jnp
import numpy as np
from jax import lax
from jax.experimental import pallas as pl
from jax.experimental.pallas import tpu as pltpu

D_MODEL = 1024
DEPTH = 2
A_WIDTH = 512
HEAD_DIM = 128
N_HEADS = A_WIDTH // HEAD_DIM
B_WIDTH = 512
N_GROUPS = 4
GROUP_DIM = B_WIDTH // N_GROUPS
D_IN = 4 * A_WIDTH + 2 * B_WIDTH
D_FF = int(math.ceil(8 * D_MODEL / 3 / 128)) * 128
SPATIAL_CHUNK = 128
RECUR_CHUNK = 64
FFN_RES = 0.5
EPS = 1e-6
F_MIN = 1e-20

SUBLANES = 8
LEVELS = (32, 16, 8, 4, 2)
N_SPLIT = 3
FFN_TILE = 512
MIX_TILE = 512
PROJ_COLS = 256
VMEM_LIMIT_BYTES = 56 * 1024 * 1024

F32 = jnp.float32
BF16 = jnp.bfloat16


def _rmsnorm(x, g):
    return x * lax.rsqrt(jnp.mean(x * x, axis=-1, keepdims=True) + EPS) * g


def _dot(a, b):
    return jnp.dot(a, b, preferred_element_type=F32)


def _dot_nt(a, b):
    return lax.dot_general(a, b, (((1,), (1,)), ((), ())), preferred_element_type=F32)


def _dot_tn(a, b):
    return lax.dot_general(a, b, (((0,), (0,)), ((), ())), preferred_element_type=F32)


def _ffn_kernel(x_ref, g_ref, wg_ref, wu_ref, wd_ref, gf_ref, o_ref, *, final):
    rows = [slice(i * FFN_TILE // 2, (i + 1) * FFN_TILE // 2) for i in range(2)]
    xs = [x_ref[r, :] for r in rows]
    hs = [_rmsnorm(x, g_ref[...]).astype(BF16) for x in xs]
    gated = [(_dot(h, wg_ref[...]), _dot(h, wu_ref[...])) for h in hs]
    for r, x, (gate, up) in zip(rows, xs, gated):
        act = (gate * jax.nn.sigmoid(gate) * up).astype(BF16)
        out = x + FFN_RES * _dot(act, wd_ref[...])
        if final:
            out = _rmsnorm(out, gf_ref[...])
        o_ref[r, :] = out


def _resident(shape):
    return pl.BlockSpec(shape, lambda *_: (0,) * len(shape), pipeline_mode=pl.Buffered(1))


def _ffn(x2d, g, wg, wu, wd, gf, *, final):
    m = x2d.shape[0]
    tile = pl.BlockSpec((FFN_TILE, D_MODEL), lambda i: (i, 0))
    return pl.pallas_call(
        functools.partial(_ffn_kernel, final=final),
        grid=(m // FFN_TILE,),
        in_specs=[tile, _resident((1, D_MODEL)), _resident((D_MODEL, D_FF)), _resident((D_MODEL, D_FF)),
                  _resident((D_FF, D_MODEL)), _resident((1, D_MODEL))],
        out_specs=tile,
        out_shape=jax.ShapeDtypeStruct(x2d.shape, F32),
        compiler_params=pltpu.CompilerParams(dimension_semantics=("arbitrary",),
                                             vmem_limit_bytes=VMEM_LIMIT_BYTES),
        name="ffn_final" if final else "ffn",
    )(x2d, g, wg, wu, wd, gf)


def _split3(x):
    hi = x.astype(BF16)
    r1 = x - hi.astype(F32)
    mid = r1.astype(BF16)
    lo = (r1 - mid.astype(F32)).astype(BF16)
    return jnp.concatenate([hi, mid, lo], axis=0)


def _prefix_sum_matrix():
    c = RECUR_CHUNK
    tril = (np.arange(c)[None, :] <= np.arange(c)[:, None]).astype(np.float32)
    return np.concatenate([tril] * N_SPLIT, axis=1)


def _level_ids():
    c = RECUR_CHUNK
    t = np.arange(c)[:, None]
    s = np.arange(c)[None, :]
    ids = np.full((c, c), -1, np.int32)
    ids[((t // 2) == (s // 2)) & (s <= t)] = 0
    for i, m in enumerate(LEVELS):
        pair = ((t // (2 * m)) == (s // (2 * m))) & ((t // m) % 2 == 1) & ((s // m) % 2 == 0)
        ids[pair] = 1 + i
    return np.concatenate([ids, ids], axis=1)


def _rows3(x):
    return x.reshape(RECUR_CHUNK // SUBLANES, SUBLANES, A_WIDTH)


def _odd_rows(m):
    return (lax.broadcasted_iota(jnp.int32, (1, SUBLANES, A_WIDTH), 1) // m) % 2 == 1


def _pair_operand(q, kk, x, m):
    c = RECUR_CHUNK
    if m >= SUBLANES:
        parts = [(q if b % 2 else kk)[b * m:(b + 1) * m] * x[b * m:(b + 1) * m] for b in range(c // m)]
        return jnp.concatenate(parts, axis=0)
    return (jnp.where(_odd_rows(m), _rows3(q), _rows3(kk)) * _rows3(x)).reshape(c, A_WIDTH)


def _level_decay(a, m):
    c = RECUR_CHUNK
    if m >= SUBLANES:
        parts = []
        for j in range(c // (2 * m)):
            ref = a[(2 * j + 1) * m - 1:(2 * j + 1) * m]
            parts += [ref - a[2 * j * m:(2 * j + 1) * m], a[(2 * j + 1) * m:(2 * j + 2) * m] - ref]
        return jnp.exp2(jnp.concatenate(parts, axis=0))
    a3 = _rows3(a)
    row = lax.broadcasted_iota(jnp.int32, (1, SUBLANES, A_WIDTH), 1)
    ref = a3[:, m - 1:m, :]
    for j in range(1, SUBLANES // (2 * m)):
        ref = jnp.where(row >= 2 * j * m, a3[:, (2 * j + 1) * m - 1:(2 * j + 1) * m, :], ref)
    sign = jnp.where(_odd_rows(m), 1.0, -1.0)
    return jnp.exp2((a3 - ref) * sign).reshape(c, A_WIDTH)


def _hgrn2_operands(q, z, lb, tmat):
    c = RECUR_CHUNK
    e = jnp.exp(-jnp.abs(z))
    inv = 1.0 / (1.0 + e)
    einv = e * inv
    sig_pos = jnp.where(z >= 0, inv, einv)
    sig_neg = jnp.where(z >= 0, einv, inv)
    fc = jnp.maximum(lb + (1.0 - lb) * sig_pos, F_MIN)
    kk = (1.0 - lb) * sig_neg

    a = _dot(tmat, _split3(jnp.log2(fc)))
    xa = jnp.exp2(a)
    odd = _odd_rows(1)
    return dict(
        qa=(q * xa).astype(BF16),
        kd=(kk * jnp.exp2(a[c - 1:c] - a)).astype(BF16),
        dec=xa[c - 1:c, :],
        ws=[_pair_operand(q, kk, _level_decay(a, m), m).astype(BF16) for m in LEVELS],
        q2=jnp.where(odd, _rows3(q * fc), _rows3(q)).reshape(c, A_WIDTH).astype(BF16),
        k2=jnp.where(odd, _rows3(kk / fc), _rows3(kk)).reshape(c, A_WIDTH).astype(BF16),
    )


def _head_pair(x, pi):
    return x[:, 2 * pi * HEAD_DIM:2 * (pi + 1) * HEAD_DIM]


def _block_diag(x2):
    zero = jnp.zeros((x2.shape[0], HEAD_DIM), x2.dtype)
    return jnp.concatenate([jnp.concatenate([x2[:, :HEAD_DIM], zero], axis=1),
                            jnp.concatenate([zero, x2[:, HEAD_DIM:]], axis=1)], axis=0)


def _hgrn2_products(ops, vb, states):
    new_states, scores, inter = [], [], []
    for hd in range(N_HEADS):
        sl = slice(hd * HEAD_DIM, (hd + 1) * HEAD_DIM)
        new_states.append(states[hd] * ops["dec"][:, sl] + _dot_tn(vb[:, sl], ops["kd"][:, sl]))
    for pi in range(N_HEADS // 2):
        pair_scores = [_dot_nt(_head_pair(ops["q2"], pi), _block_diag(_head_pair(ops["k2"], pi)))]
        for w in ops["ws"]:
            pair_scores.append(_dot_nt(_head_pair(w, pi), _block_diag(_head_pair(w, pi))))
        scores.append(pair_scores)
    for hd in range(N_HEADS):
        sl = slice(hd * HEAD_DIM, (hd + 1) * HEAD_DIM)
        inter.append(_dot(ops["qa"][:, sl], states[hd].T.astype(BF16)))
    return new_states, scores, inter


def _hgrn2_outputs(scores, inter, vb, g, hnorm, masks):
    outs = []
    for pi in range(N_HEADS // 2):
        p = jnp.where(masks[0], scores[pi][0], 0.0)
        for i in range(len(LEVELS)):
            p = jnp.where(masks[1 + i], scores[pi][1 + i], p)
        o2 = _dot(p.astype(BF16), _block_diag(_head_pair(vb, pi)))
        for j in range(2):
            hd = 2 * pi + j
            sl = slice(hd * HEAD_DIM, (hd + 1) * HEAD_DIM)
            o = inter[hd] + o2[:, j * HEAD_DIM:(j + 1) * HEAD_DIM]
            o = o * lax.rsqrt(jnp.mean(o * o, axis=-1, keepdims=True) + EPS) * hnorm[:, sl]
            gh = g[:, sl]
            outs.append((o * (gh * jax.nn.sigmoid(gh))).astype(BF16))
    return jnp.concatenate(outs, axis=1)


def _gelu_tanh(x):
    c1 = math.sqrt(2.0 / math.pi)
    return x * (0.5 + 0.5 * jnp.tanh(x * (c1 + (c1 * 0.044715) * (x * x))))


def _mixer_kernel(x_ref, gmix_ref, win_ref, lbp_ref, hnorm_ref, lng_ref, lnb_ref, wsp_ref, bsp_ref, wout_ref,
                  tmat_ref, lvl_ref, o_ref, state_ref, mix_ref, *, layer):
    c = RECUR_CHUNK
    n_chunks = MIX_TILE // c

    @pl.when(pl.program_id(1) == 0)
    def _():
        state_ref[...] = jnp.zeros(state_ref.shape, F32)

    halves = [slice(i * MIX_TILE // 2, (i + 1) * MIX_TILE // 2) for i in range(2)]
    xs = [x_ref[0, r, :] for r in halves]
    hs = [_rmsnorm(xh, gmix_ref[...]).astype(BF16) for xh in xs]
    h = jnp.concatenate(hs, axis=0)
    proj = {}

    def project(k, j, by_halves=False):
        lo = k * A_WIDTH + j * PROJ_COLS
        w = win_ref[:, lo:lo + PROJ_COLS]
        proj[k, j] = jnp.concatenate([_dot(hh, w) for hh in hs], axis=0) if by_halves else _dot(h, w)

    def projected(k, rows):
        return jnp.concatenate([proj[k, j][rows] for j in range(A_WIDTH // PROJ_COLS)], axis=1)

    lp = lbp_ref[...]
    ex = jnp.exp(lp - jnp.max(lp, axis=0, keepdims=True))
    prob = ex / jnp.sum(ex, axis=0, keepdims=True)
    lbs = jnp.sum(prob[:layer + 1, :], axis=0, keepdims=True) - prob[0:1, :]
    hnorm = hnorm_ref[...]
    tmat = tmat_ref[...]
    lvl = lvl_ref[...]
    masks = [lvl == i for i in range(1 + len(LEVELS))]

    p = SPATIAL_CHUNK
    prow = lax.broadcasted_iota(jnp.int32, (p, p), 0)
    pcol = lax.broadcasted_iota(jnp.int32, (p, p), 1)
    w_tril = [jnp.where(prow >= pcol, wsp_ref[gi], 0.0).astype(BF16) for gi in range(N_GROUPS)]
    lng = lng_ref[...]
    lnb = lnb_ref[...]
    bias = bsp_ref[...]

    def spatial_gates(si):
        rows = slice(si * p, (si + 1) * p)
        u = _gelu_tanh(projected(4, rows))
        vv = _gelu_tanh(projected(5, rows))
        mu = jnp.mean(vv, axis=-1, keepdims=True)
        var = jnp.mean(jnp.square(vv - mu), axis=-1, keepdims=True)
        return u, ((vv - mu) * lax.rsqrt(var + EPS) * lng + lnb).astype(BF16)

    def spatial_mix(si, u, vn):
        rows = slice(si * p, (si + 1) * p)
        for gi in range(N_GROUPS):
            lo, hi = gi * GROUP_DIM, (gi + 1) * GROUP_DIM
            s = _dot(w_tril[gi], vn[:, lo:hi]) + bias[:, lo:hi]
            mix_ref[rows, A_WIDTH + lo:A_WIDTH + hi] = (u[:, lo:hi] * s).astype(BF16)

    n_spatial = MIX_TILE // p
    project(5, 0, by_halves=True)
    project(5, 1)
    project(4, 0)
    project(4, 1)
    gated = []
    for si, (k, j) in enumerate(((1, 0), (1, 1), (0, 0), (0, 1))):
        project(k, j)
        gated.append(spatial_gates(si))
    for si, (k, j) in enumerate(((2, 0), (2, 1), (3, 0), (3, 1))):
        project(k, j)
        spatial_mix(si, *gated[si])

    def field(ci, k):
        return projected(k, slice(ci * c, (ci + 1) * c))

    operands, values, products = {}, {}, {}

    def prepare(ci):
        operands[ci] = _hgrn2_operands(field(ci, 0), field(ci, 1), lbs, tmat)
        values[ci] = field(ci, 2).astype(BF16)

    def finish(ci):
        scores, inter = products.pop(ci)
        mix_ref[ci * c:(ci + 1) * c, 0:A_WIDTH] = _hgrn2_outputs(scores, inter, values.pop(ci), field(ci, 3),
                                                                  hnorm, masks)

    states = [state_ref[hd] for hd in range(N_HEADS)]
    prepare(0)
    prepare(1)
    for ci in range(n_chunks):
        states, scores, inter = _hgrn2_products(operands.pop(ci), values[ci], states)
        products[ci] = (scores, inter)
        if ci >= 1:
            finish(ci - 1)
        if ci + 2 < n_chunks:
            prepare(ci + 2)
    finish(n_chunks - 1)
    for hd in range(N_HEADS):
        state_ref[hd] = states[hd]

    for r, xh in zip(halves, xs):
        o_ref[0, r, :] = xh + _dot(mix_ref[r, :], wout_ref[...])


def _mixer(x, gmix, win, lbp, hnorm, lng, lnb, wsp, bsp, wout, *, layer):
    bsz, seq, _ = x.shape
    c = RECUR_CHUNK
    tile = pl.BlockSpec((1, MIX_TILE, D_MODEL), lambda b, s: (b, s, 0))
    tmat = jnp.asarray(_prefix_sum_matrix(), BF16)
    lvl = jnp.asarray(_level_ids())
    return pl.pallas_call(
        functools.partial(_mixer_kernel, layer=layer),
        grid=(bsz, seq // MIX_TILE),
        in_specs=[tile, _resident((1, D_MODEL)), _resident((D_MODEL, D_IN)), _resident((DEPTH, A_WIDTH)),
                  _resident((1, A_WIDTH)), _resident((1, B_WIDTH)), _resident((1, B_WIDTH)),
                  _resident((N_GROUPS, SPATIAL_CHUNK, SPATIAL_CHUNK)), _resident((SPATIAL_CHUNK, B_WIDTH)),
                  _resident((D_MODEL, D_MODEL)), _resident((c, N_SPLIT * c)), _resident((c, 2 * c))],
        out_specs=tile,
        out_shape=jax.ShapeDtypeStruct(x.shape, F32),
        scratch_shapes=[pltpu.VMEM((N_HEADS, HEAD_DIM, HEAD_DIM), F32),
                        pltpu.VMEM((MIX_TILE, D_MODEL), BF16)],
        compiler_params=pltpu.CompilerParams(dimension_semantics=("arbitrary", "arbitrary"),
                                             vmem_limit_bytes=VMEM_LIMIT_BYTES),
        name="mixer",
    )(x, gmix, win, lbp, hnorm, lng, lnb, wsp, bsp, wout, tmat, lvl)


def kernel(x, norm_ffn1, ffn1_w_gate, ffn1_w_up, ffn1_w_down, norm_mix, w_in, lb_param, hgrn_norm, ln_v_gain,
           ln_v_bias, w_spatial, b_spatial, w_out, norm_ffn2, ffn2_w_gate, ffn2_w_up, ffn2_w_down, norm_final):
    bsz, seq, d = x.shape
    row = lambda t: t.reshape(1, -1).astype(F32)
    lbp = lb_param.astype(F32)
    gf = row(norm_final)
    for l in range(DEPTH):
        x = _ffn(x.reshape(bsz * seq, d), row(norm_ffn1[l]), ffn1_w_gate[l].astype(BF16),
                 ffn1_w_up[l].astype(BF16), ffn1_w_down[l].astype(BF16), gf, final=False).reshape(bsz, seq, d)
        bsp = jnp.repeat(b_spatial[l].astype(F32).T, GROUP_DIM, axis=1)
        x = _mixer(x, row(norm_mix[l]), w_in[l].astype(BF16), lbp, row(hgrn_norm[l]), row(ln_v_gain[l]),
                   row(ln_v_bias[l]), w_spatial[l].astype(F32), bsp, w_out[l].astype(BF16), layer=l)
        x = _ffn(x.reshape(bsz * seq, d), row(norm_ffn2[l]), ffn2_w_gate[l].astype(BF16),
                 ffn2_w_up[l].astype(BF16), ffn2_w_down[l].astype(BF16), gf,
                 final=(l == DEPTH - 1)).reshape(bsz, seq, d)
    return x
```

```python
import functools
import math

import jax
import jax.numpy as jnp
import numpy as np
from jax import lax
from jax.experimental import pallas as pl
from jax.experimental.pallas import tpu as pltpu

D_MODEL = 1024
DEPTH = 2
A_WIDTH = 512
HEAD_DIM = 128
N_HEADS = A_WIDTH // HEAD_DIM
B_WIDTH = 512
N_GROUPS = 4
GROUP_DIM = B_WIDTH // N_GROUPS
D_IN = 4 * A_WIDTH + 2 * B_WIDTH
D_FF = int(math.ceil(8 * D_MODEL / 3 / 128)) * 128
SPATIAL_CHUNK = 128
RECUR_CHUNK = 64
FFN_RES = 0.5
EPS = 1e-6
F_MIN = 1e-20

SUBLANES = 8
LEVELS = (32, 16, 8, 4, 2)
N_SPLIT = 3
FFN_TILE = 1024
FFN_ROWS = 256
MIX_TILE = 1024
PROJ_COLS = 256
PREPARE_AHEAD = 8
VMEM_LIMIT_BYTES = 56 * 1024 * 1024

F32 = jnp.float32
BF16 = jnp.bfloat16


def _rmsnorm(x, g):
    return x * lax.rsqrt(jnp.mean(x * x, axis=-1, keepdims=True) + EPS) * g


def _dot(a, b):
    return jnp.dot(a, b, preferred_element_type=F32)


def _dot_nt(a, b):
    return lax.dot_general(a, b, (((1,), (1,)), ((), ())), preferred_element_type=F32)


def _dot_tn(a, b):
    return lax.dot_general(a, b, (((0,), (0,)), ((), ())), preferred_element_type=F32)


def _ffn_kernel(x_ref, g_ref, wg_ref, wu_ref, wd_ref, gf_ref, o_ref, *, final):
    n_groups = FFN_TILE // FFN_ROWS
    rows = [slice(i * FFN_ROWS, (i + 1) * FFN_ROWS) for i in range(n_groups)]

    def gate_up(i):
        h = _rmsnorm(x_ref[rows[i], :], g_ref[...]).astype(BF16)
        return _dot(h, wg_ref[...]), _dot(h, wu_ref[...])

    def down(i, gate, up):
        act = (gate * jax.nn.sigmoid(gate) * up).astype(BF16)
        out = x_ref[rows[i], :] + FFN_RES * _dot(act, wd_ref[...])
        if final:
            out = _rmsnorm(out, gf_ref[...])
        o_ref[rows[i], :] = out

    pending = gate_up(0)
    for i in range(1, n_groups):
        nxt = gate_up(i)
        down(i - 1, *pending)
        pending = nxt
    down(n_groups - 1, *pending)


def _resident(shape):
    return pl.BlockSpec(shape, lambda *_: (0,) * len(shape), pipeline_mode=pl.Buffered(1))


def _ffn(x2d, g, wg, wu, wd, gf, *, final):
    m = x2d.shape[0]
    tile = pl.BlockSpec((FFN_TILE, D_MODEL), lambda i: (i, 0))
    return pl.pallas_call(
        functools.partial(_ffn_kernel, final=final),
        grid=(m // FFN_TILE,),
        in_specs=[tile, _resident((1, D_MODEL)), _resident((D_MODEL, D_FF)), _resident((D_MODEL, D_FF)),
                  _resident((D_FF, D_MODEL)), _resident((1, D_MODEL))],
        out_specs=tile,
        out_shape=jax.ShapeDtypeStruct(x2d.shape, F32),
        compiler_params=pltpu.CompilerParams(dimension_semantics=("arbitrary",),
                                             vmem_limit_bytes=VMEM_LIMIT_BYTES),
        name="ffn_final" if final else "ffn",
    )(x2d, g, wg, wu, wd, gf)


def _split3(x):
    hi = x.astype(BF16)
    r1 = x - hi.astype(F32)
    mid = r1.astype(BF16)
    lo = (r1 - mid.astype(F32)).astype(BF16)
    return jnp.concatenate([hi, mid, lo], axis=0)


def _prefix_sum_matrix():
    c = RECUR_CHUNK
    tril = (np.arange(c)[None, :] <= np.arange(c)[:, None]).astype(np.float32)
    return np.concatenate([tril] * N_SPLIT, axis=1)


def _level_ids():
    c = RECUR_CHUNK
    t = np.arange(c)[:, None]
    s = np.arange(c)[None, :]
    ids = np.full((c, c), -1, np.int32)
    ids[((t // 2) == (s // 2)) & (s <= t)] = 0
    for i, m in enumerate(LEVELS):
        pair = ((t // (2 * m)) == (s // (2 * m))) & ((t // m) % 2 == 1) & ((s // m) % 2 == 0)
        ids[pair] = 1 + i
    return np.concatenate([ids, ids], axis=1)


def _rows3(x):
    return x.reshape(RECUR_CHUNK // SUBLANES, SUBLANES, A_WIDTH)


def _odd_rows(m):
    return (lax.broadcasted_iota(jnp.int32, (1, SUBLANES, A_WIDTH), 1) // m) % 2 == 1


def _pair_operand(q, kk, x, m):
    c = RECUR_CHUNK
    if m >= SUBLANES:
        parts = [(q if b % 2 else kk)[b * m:(b + 1) * m] * x[b * m:(b + 1) * m] for b in range(c // m)]
        return jnp.concatenate(parts, axis=0)
    return (jnp.where(_odd_rows(m), _rows3(q), _rows3(kk)) * _rows3(x)).reshape(c, A_WIDTH)


def _level_decay(a, m):
    c = RECUR_CHUNK
    if m >= SUBLANES:
        parts = []
        for j in range(c // (2 * m)):
            ref = a[(2 * j + 1) * m - 1:(2 * j + 1) * m]
            parts += [ref - a[2 * j * m:(2 * j + 1) * m], a[(2 * j + 1) * m:(2 * j + 2) * m] - ref]
        return jnp.exp2(jnp.concatenate(parts, axis=0))
    a3 = _rows3(a)
    row = lax.broadcasted_iota(jnp.int32, (1, SUBLANES, A_WIDTH), 1)
    ref = a3[:, m - 1:m, :]
    for j in range(1, SUBLANES // (2 * m)):
        ref = jnp.where(row >= 2 * j * m, a3[:, (2 * j + 1) * m - 1:(2 * j + 1) * m, :], ref)
    sign = jnp.where(_odd_rows(m), 1.0, -1.0)
    return jnp.exp2((a3 - ref) * sign).reshape(c, A_WIDTH)


def _hgrn2_operands(q, z, lb, tmat):
    c = RECUR_CHUNK
    e = jnp.exp(-jnp.abs(z))
    inv = 1.0 / (1.0 + e)
    einv = e * inv
    sig_pos = jnp.where(z >= 0, inv, einv)
    sig_neg = jnp.where(z >= 0, einv, inv)
    fc = jnp.maximum(lb + (1.0 - lb) * sig_pos, F_MIN)
    kk = (1.0 - lb) * sig_neg

    a = _dot(tmat, _split3(jnp.log2(fc)))
    xa = jnp.exp2(a)
    odd = _odd_rows(1)
    return dict(
        qa=(q * xa).astype(BF16),
        kd=(kk * jnp.exp2(a[c - 1:c] - a)).astype(BF16),
        dec=xa[c - 1:c, :],
        ws=[_pair_operand(q, kk, _level_decay(a, m), m).astype(BF16) for m in LEVELS],
        q2=jnp.where(odd, _rows3(q * fc), _rows3(q)).reshape(c, A_WIDTH).astype(BF16),
        k2=jnp.where(odd, _rows3(kk / fc), _rows3(kk)).reshape(c, A_WIDTH).astype(BF16),
    )


def _head_pair(x, pi):
    return x[:, 2 * pi * HEAD_DIM:2 * (pi + 1) * HEAD_DIM]


def _block_diag(x2):
    zero = jnp.zeros((x2.shape[0], HEAD_DIM), x2.dtype)
    return jnp.concatenate([jnp.concatenate([x2[:, :HEAD_DIM], zero], axis=1),
                            jnp.concatenate([zero, x2[:, HEAD_DIM:]], axis=1)], axis=0)


def _hgrn2_products(ops, vb, states):
    new_states, scores, inter = [], [], []
    for hd in range(N_HEADS):
        sl = slice(hd * HEAD_DIM, (hd + 1) * HEAD_DIM)
        new_states.append(states[hd] * ops["dec"][:, sl] + _dot_tn(vb[:, sl], ops["kd"][:, sl]))
    for pi in range(N_HEADS // 2):
        pair_scores = [_dot_nt(_head_pair(ops["q2"], pi), _block_diag(_head_pair(ops["k2"], pi)))]
        for w in ops["ws"]:
            pair_scores.append(_dot_nt(_head_pair(w, pi), _block_diag(_head_pair(w, pi))))
        scores.append(pair_scores)
    for hd in range(N_HEADS):
        sl = slice(hd * HEAD_DIM, (hd + 1) * HEAD_DIM)
        inter.append(_dot(ops["qa"][:, sl], states[hd].T.astype(BF16)))
    return new_states, scores, inter


def _hgrn2_outputs(scores, inter, vb, g, hnorm, masks):
    outs = []
    for pi in range(N_HEADS // 2):
        p = jnp.where(masks[0], scores[pi][0], 0.0)
        for i in range(len(LEVELS)):
            p = jnp.where(masks[1 + i], scores[pi][1 + i], p)
        o2 = _dot(p.astype(BF16), _block_diag(_head_pair(vb, pi)))
        for j in range(2):
            hd = 2 * pi + j
            sl = slice(hd * HEAD_DIM, (hd + 1) * HEAD_DIM)
            o = inter[hd] + o2[:, j * HEAD_DIM:(j + 1) * HEAD_DIM]
            o = o * lax.rsqrt(jnp.mean(o * o, axis=-1, keepdims=True) + EPS) * hnorm[:, sl]
            gh = g[:, sl]
            outs.append((o * (gh * jax.nn.sigmoid(gh))).astype(BF16))
    return jnp.concatenate(outs, axis=1)


def _gelu_tanh(x):
    c1 = math.sqrt(2.0 / math.pi)
    return x * (0.5 + 0.5 * jnp.tanh(x * (c1 + (c1 * 0.044715) * (x * x))))


def _mixer_kernel(x_ref, gmix_ref, win_ref, lbp_ref, hnorm_ref, lng_ref, lnb_ref, wsp_ref, bsp_ref, wout_ref,
                  tmat_ref, lvl_ref, o_ref, state_ref, mix_ref, *, layer):
    c = RECUR_CHUNK
    n_chunks = MIX_TILE // c

    @pl.when(pl.program_id(1) == 0)
    def _():
        state_ref[...] = jnp.zeros(state_ref.shape, F32)

    halves = [slice(i * MIX_TILE // 2, (i + 1) * MIX_TILE // 2) for i in range(2)]
    xs = [x_ref[0, r, :] for r in halves]
    hs = [_rmsnorm(xh, gmix_ref[...]).astype(BF16) for xh in xs]
    h = jnp.concatenate(hs, axis=0)
    proj = {}

    def project(k, j, by_halves=False):
        lo = k * A_WIDTH + j * PROJ_COLS
        w = win_ref[:, lo:lo + PROJ_COLS]
        proj[k, j] = jnp.concatenate([_dot(hh, w) for hh in hs], axis=0) if by_halves else _dot(h, w)

    def projected(k, rows):
        return jnp.concatenate([proj[k, j][rows] for j in range(A_WIDTH // PROJ_COLS)], axis=1)

    lp = lbp_ref[...]
    ex = jnp.exp(lp - jnp.max(lp, axis=0, keepdims=True))
    prob = ex / jnp.sum(ex, axis=0, keepdims=True)
    lbs = jnp.sum(prob[:layer + 1, :], axis=0, keepdims=True) - prob[0:1, :]
    hnorm = hnorm_ref[...]
    tmat = tmat_ref[...]
    lvl = lvl_ref[...]
    masks = [lvl == i for i in range(1 + len(LEVELS))]

    p = SPATIAL_CHUNK
    prow = lax.broadcasted_iota(jnp.int32, (p, p), 0)
    pcol = lax.broadcasted_iota(jnp.int32, (p, p), 1)
    w_tril = [jnp.where(prow >= pcol, wsp_ref[gi], 0.0).astype(BF16) for gi in range(N_GROUPS)]
    lng = lng_ref[...]
    lnb = lnb_ref[...]
    bias = bsp_ref[...]

    def spatial_gates(si):
        rows = slice(si * p, (si + 1) * p)
        u = _gelu_tanh(projected(4, rows))
        vv = _gelu_tanh(projected(5, rows))
        mu = jnp.mean(vv, axis=-1, keepdims=True)
        var = jnp.mean(jnp.square(vv - mu), axis=-1, keepdims=True)
        return u, ((vv - mu) * lax.rsqrt(var + EPS) * lng + lnb).astype(BF16)

    def spatial_mix(si, u, vn):
        rows = slice(si * p, (si + 1) * p)
        for gi in range(N_GROUPS):
            lo, hi = gi * GROUP_DIM, (gi + 1) * GROUP_DIM
            s = _dot(w_tril[gi], vn[:, lo:hi]) + bias[:, lo:hi]
            mix_ref[rows, A_WIDTH + lo:A_WIDTH + hi] = (u[:, lo:hi] * s).astype(BF16)

    n_spatial = MIX_TILE // p
    project(5, 0, by_halves=True)
    project(5, 1)
    project(4, 0)
    project(4, 1)
    def field(ci, k):
        return projected(k, slice(ci * c, (ci + 1) * c))

    operands, products = {}, {}

    def prepare(ci):
        operands[ci] = _hgrn2_operands(field(ci, 0), field(ci, 1), lbs, tmat)

    def finish(ci):
        scores, inter, vb = products.pop(ci)
        mix_ref[ci * c:(ci + 1) * c, 0:A_WIDTH] = _hgrn2_outputs(scores, inter, vb, field(ci, 3), hnorm, masks)

    gated = []
    slots = ((1, 0), (1, 1), (0, 0), (0, 1))
    for i, (k, j) in enumerate(slots):
        project(k, j)
        for si in range(i * n_spatial // len(slots), (i + 1) * n_spatial // len(slots)):
            gated.append(spatial_gates(si))
    slots = ((2, 0), (2, 1), (3, 0), (3, 1))
    for i, (k, j) in enumerate(slots):
        project(k, j)
        for ci in range(i * PREPARE_AHEAD // len(slots), (i + 1) * PREPARE_AHEAD // len(slots)):
            prepare(ci)
    for si in range(n_spatial):
        spatial_mix(si, *gated[si])

    states = [state_ref[hd] for hd in range(N_HEADS)]
    for ci in range(n_chunks):
        vb = field(ci, 2).astype(BF16)
        states, scores, inter = _hgrn2_products(operands.pop(ci), vb, states)
        products[ci] = (scores, inter, vb)
        if ci >= 1:
            finish(ci - 1)
        if ci + PREPARE_AHEAD < n_chunks:
            prepare(ci + PREPARE_AHEAD)
    finish(n_chunks - 1)
    for hd in range(N_HEADS):
        state_ref[hd] = states[hd]

    for r, xh in zip(halves, xs):
        o_ref[0, r, :] = xh + _dot(mix_ref[r, :], wout_ref[...])


def _mixer(x, gmix, win, lbp, hnorm, lng, lnb, wsp, bsp, wout, *, layer):
    bsz, seq, _ = x.shape
    c = RECUR_CHUNK
    tile = pl.BlockSpec((1, MIX_TILE, D_MODEL), lambda b, s: (b, s, 0))
    tmat = jnp.asarray(_prefix_sum_matrix(), BF16)
    lvl = jnp.asarray(_level_ids())
    return pl.pallas_call(
        functools.partial(_mixer_kernel, layer=layer),
        grid=(bsz, seq // MIX_TILE),
        in_specs=[tile, _resident((1, D_MODEL)), _resident((D_MODEL, D_IN)), _resident((DEPTH, A_WIDTH)),
                  _resident((1, A_WIDTH)), _resident((1, B_WIDTH)), _resident((1, B_WIDTH)),
                  _resident((N_GROUPS, SPATIAL_CHUNK, SPATIAL_CHUNK)), _resident((SPATIAL_CHUNK, B_WIDTH)),
                  _resident((D_MODEL, D_MODEL)), _resident((c, N_SPLIT * c)), _resident((c, 2 * c))],
        out_specs=tile,
        out_shape=jax.ShapeDtypeStruct(x.shape, F32),
        scratch_shapes=[pltpu.VMEM((N_HEADS, HEAD_DIM, HEAD_DIM), F32),
                        pltpu.VMEM((MIX_TILE, D_MODEL), BF16)],
        compiler_params=pltpu.CompilerParams(dimension_semantics=("arbitrary", "arbitrary"),
                                             vmem_limit_bytes=VMEM_LIMIT_BYTES),
        name="mixer",
    )(x, gmix, win, lbp, hnorm, lng, lnb, wsp, bsp, wout, tmat, lvl)


def kernel(x, norm_ffn1, ffn1_w_gate, ffn1_w_up, ffn1_w_down, norm_mix, w_in, lb_param, hgrn_norm, ln_v_gain,
           ln_v_bias, w_spatial, b_spatial, w_out, norm_ffn2, ffn2_w_gate, ffn2_w_up, ffn2_w_down, norm_final):
    bsz, seq, d = x.shape
    row = lambda t: t.reshape(1, -1).astype(F32)
    lbp = lb_param.astype(F32)
    gf = row(norm_final)
    for l in range(DEPTH):
        x = _ffn(x.reshape(bsz * seq, d), row(norm_ffn1[l]), ffn1_w_gate[l].astype(BF16),
                 ffn1_w_up[l].astype(BF16), ffn1_w_down[l].astype(BF16), gf, final=False).reshape(bsz, seq, d)
        bsp = jnp.repeat(b_spatial[l].astype(F32).T, GROUP_DIM, axis=1)
        x = _mixer(x, row(norm_mix[l]), w_in[l].astype(BF16), lbp, row(hgrn_norm[l]), row(ln_v_gain[l]),
                   row(ln_v_bias[l]), w_spatial[l].astype(F32), bsp, w_out[l].astype(BF16), layer=l)
        x = _ffn(x.reshape(bsz * seq, d), row(norm_ffn2[l]), ffn2_w_gate[l].astype(BF16),
                 ffn2_w_up[l].astype(BF16), ffn2_w_down[l].astype(BF16), gf,
                 final=(l == DEPTH - 1)).reshape(bsz, seq, d)
    return x
```

```python
import functools
import math

import jax
import jax.numpy as jnp
import numpy as np
from jax import lax
from jax.experimental import pallas as pl
from jax.experimental.pallas import tpu as pltpu

D_MODEL = 1024
DEPTH = 2
A_WIDTH = 512
HEAD_DIM = 128
N_HEADS = A_WIDTH // HEAD_DIM
B_WIDTH = 512
N_GROUPS = 4
GROUP_DIM = B_WIDTH // N_GROUPS
D_IN = 4 * A_WIDTH + 2 * B_WIDTH
D_FF = int(math.ceil(8 * D_MODEL / 3 / 128)) * 128
SPATIAL_CHUNK = 128
RECUR_CHUNK = 64
FFN_RES = 0.5
EPS = 1e-6
F_MIN = 1e-20

SUBLANES = 8
LEVELS = (32, 16, 8, 4, 2)
N_SPLIT = 3
FFN_TILE = 1024
FFN_ROWS = 256
STAGE_ELEMS = 3 * 128 * 1024
MIX_TILE = 1024
PROJ_COLS = 256
PREPARE_AHEAD = 8
VMEM_LIMIT_BYTES = 56 * 1024 * 1024

F32 = jnp.float32
BF16 = jnp.bfloat16


def _rmsnorm(x, g):
    return x * lax.rsqrt(jnp.mean(x * x, axis=-1, keepdims=True) + EPS) * g


def _dot(a, b):
    return jnp.dot(a, b, preferred_element_type=F32)


def _dot_nt(a, b):
    return lax.dot_general(a, b, (((1,), (1,)), ((), ())), preferred_element_type=F32)


def _dot_tn(a, b):
    return lax.dot_general(a, b, (((0,), (0,)), ((), ())), preferred_element_type=F32)


def _hbm():
    return pl.BlockSpec(memory_space=pl.ANY)


def _stage(rows, cols):
    n = max(r for r in range(SUBLANES, rows + 1, SUBLANES) if rows % r == 0 and r * cols <= STAGE_ELEMS)
    return [pltpu.VMEM((2, n, cols), F32), pltpu.SemaphoreType.DMA((2,))]


def _load_weights(jobs):
    chunks = [(src, dst, stage, sems, r0)
              for src, dst, stage, sems in jobs for r0 in range(0, dst.shape[0], stage.shape[1])]

    def copy(i):
        src, _, stage, sems, r0 = chunks[i]
        return pltpu.make_async_copy(src.at[pl.ds(r0, stage.shape[1]), :], stage.at[i % 2], sems.at[i % 2])

    copy(0).start()
    for i, (_, dst, stage, _, r0) in enumerate(chunks):
        if i + 1 < len(chunks):
            copy(i + 1).start()
        copy(i).wait()
        dst[r0:r0 + stage.shape[1], :] = stage[i % 2].astype(BF16)


def _ffn_kernel(x_ref, g_ref, wg_hbm, wu_hbm, wd_hbm, gf_ref, o_ref, wg_ref, wu_ref, wd_ref, stage_in, sem_in,
                stage_out, sem_out, *, layer, final):
    @pl.when(pl.program_id(0) == 0)
    def _():
        _load_weights([(wg_hbm.at[layer], wg_ref, stage_in, sem_in), (wu_hbm.at[layer], wu_ref, stage_in, sem_in),
                       (wd_hbm.at[layer], wd_ref, stage_out, sem_out)])

    n_groups = FFN_TILE // FFN_ROWS
    rows = [slice(i * FFN_ROWS, (i + 1) * FFN_ROWS) for i in range(n_groups)]

    def gate_up(i):
        h = _rmsnorm(x_ref[rows[i], :], g_ref[...]).astype(BF16)
        return _dot(h, wg_ref[...]), _dot(h, wu_ref[...])

    def down(i, gate, up):
        act = (gate * jax.nn.sigmoid(gate) * up).astype(BF16)
        out = x_ref[rows[i], :] + FFN_RES * _dot(act, wd_ref[...])
        if final:
            out = _rmsnorm(out, gf_ref[...])
        o_ref[rows[i], :] = out

    pending = gate_up(0)
    for i in range(1, n_groups):
        nxt = gate_up(i)
        down(i - 1, *pending)
        pending = nxt
    down(n_groups - 1, *pending)


def _resident(shape):
    return pl.BlockSpec(shape, lambda *_: (0,) * len(shape), pipeline_mode=pl.Buffered(1))


def _ffn(x2d, g, wg, wu, wd, gf, *, layer, final):
    m = x2d.shape[0]
    tile = pl.BlockSpec((FFN_TILE, D_MODEL), lambda i: (i, 0))
    return pl.pallas_call(
        functools.partial(_ffn_kernel, layer=layer, final=final),
        grid=(m // FFN_TILE,),
        in_specs=[tile, _resident((1, D_MODEL)), _hbm(), _hbm(), _hbm(), _resident((1, D_MODEL))],
        out_specs=tile,
        out_shape=jax.ShapeDtypeStruct(x2d.shape, F32),
        scratch_shapes=[pltpu.VMEM((D_MODEL, D_FF), BF16), pltpu.VMEM((D_MODEL, D_FF), BF16),
                        pltpu.VMEM((D_FF, D_MODEL), BF16),
                        *_stage(D_MODEL, D_FF), *_stage(D_FF, D_MODEL)],
        compiler_params=pltpu.CompilerParams(dimension_semantics=("arbitrary",),
                                             vmem_limit_bytes=VMEM_LIMIT_BYTES),
        name="ffn_final" if final else "ffn",
    )(x2d, g, wg, wu, wd, gf)


def _split3(x):
    hi = x.astype(BF16)
    r1 = x - hi.astype(F32)
    mid = r1.astype(BF16)
    lo = (r1 - mid.astype(F32)).astype(BF16)
    return jnp.concatenate([hi, mid, lo], axis=0)


def _prefix_sum_matrix():
    c = RECUR_CHUNK
    tril = (np.arange(c)[None, :] <= np.arange(c)[:, None]).astype(np.float32)
    return np.concatenate([tril] * N_SPLIT, axis=1)


def _level_ids():
    c = RECUR_CHUNK
    t = np.arange(c)[:, None]
    s = np.arange(c)[None, :]
    ids = np.full((c, c), -1, np.int32)
    ids[((t // 2) == (s // 2)) & (s <= t)] = 0
    for i, m in enumerate(LEVELS):
        pair = ((t // (2 * m)) == (s // (2 * m))) & ((t // m) % 2 == 1) & ((s // m) % 2 == 0)
        ids[pair] = 1 + i
    return np.concatenate([ids, ids], axis=1)


def _rows3(x):
    return x.reshape(RECUR_CHUNK // SUBLANES, SUBLANES, A_WIDTH)


def _odd_rows(m):
    return (lax.broadcasted_iota(jnp.int32, (1, SUBLANES, A_WIDTH), 1) // m) % 2 == 1


def _pair_operand(q, kk, x, m):
    c = RECUR_CHUNK
    if m >= SUBLANES:
        parts = [(q if b % 2 else kk)[b * m:(b + 1) * m] * x[b * m:(b + 1) * m] for b in range(c // m)]
        return jnp.concatenate(parts, axis=0)
    return (jnp.where(_odd_rows(m), _rows3(q), _rows3(kk)) * _rows3(x)).reshape(c, A_WIDTH)


def _level_decay(a, m):
    c = RECUR_CHUNK
    if m >= SUBLANES:
        parts = []
        for j in range(c // (2 * m)):
            ref = a[(2 * j + 1) * m - 1:(2 * j + 1) * m]
            parts += [ref - a[2 * j * m:(2 * j + 1) * m], a[(2 * j + 1) * m:(2 * j + 2) * m] - ref]
        return jnp.exp2(jnp.concatenate(parts, axis=0))
    a3 = _rows3(a)
    row = lax.broadcasted_iota(jnp.int32, (1, SUBLANES, A_WIDTH), 1)
    ref = a3[:, m - 1:m, :]
    for j in range(1, SUBLANES // (2 * m)):
        ref = jnp.where(row >= 2 * j * m, a3[:, (2 * j + 1) * m - 1:(2 * j + 1) * m, :], ref)
    sign = jnp.where(_odd_rows(m), 1.0, -1.0)
    return jnp.exp2((a3 - ref) * sign).reshape(c, A_WIDTH)


def _hgrn2_operands(q, z, lb, tmat):
    c = RECUR_CHUNK
    e = jnp.exp(-jnp.abs(z))
    inv = 1.0 / (1.0 + e)
    einv = e * inv
    sig_pos = jnp.where(z >= 0, inv, einv)
    sig_neg = jnp.where(z >= 0, einv, inv)
    fc = jnp.maximum(lb + (1.0 - lb) * sig_pos, F_MIN)
    kk = (1.0 - lb) * sig_neg

    a = _dot(tmat, _split3(jnp.log2(fc)))
    xa = jnp.exp2(a)
    odd = _odd_rows(1)
    return dict(
        qa=(q * xa).astype(BF16),
        kd=(kk * jnp.exp2(a[c - 1:c] - a)).astype(BF16),
        dec=xa[c - 1:c, :],
        ws=[_pair_operand(q, kk, _level_decay(a, m), m).astype(BF16) for m in LEVELS],
        q2=jnp.where(odd, _rows3(q * fc), _rows3(q)).reshape(c, A_WIDTH).astype(BF16),
        k2=jnp.where(odd, _rows3(kk / fc), _rows3(kk)).reshape(c, A_WIDTH).astype(BF16),
    )


def _head_pair(x, pi):
    return x[:, 2 * pi * HEAD_DIM:2 * (pi + 1) * HEAD_DIM]


def _block_diag(x2):
    zero = jnp.zeros((x2.shape[0], HEAD_DIM), x2.dtype)
    return jnp.concatenate([jnp.concatenate([x2[:, :HEAD_DIM], zero], axis=1),
                            jnp.concatenate([zero, x2[:, HEAD_DIM:]], axis=1)], axis=0)


def _hgrn2_products(ops, vb, states):
    new_states, scores, inter = [], [], []
    for hd in range(N_HEADS):
        sl = slice(hd * HEAD_DIM, (hd + 1) * HEAD_DIM)
        new_states.append(states[hd] * ops["dec"][:, sl] + _dot_tn(vb[:, sl], ops["kd"][:, sl]))
    for pi in range(N_HEADS // 2):
        pair_scores = [_dot_nt(_head_pair(ops["q2"], pi), _block_diag(_head_pair(ops["k2"], pi)))]
        for w in ops["ws"]:
            pair_scores.append(_dot_nt(_head_pair(w, pi), _block_diag(_head_pair(w, pi))))
        scores.append(pair_scores)
    for hd in range(N_HEADS):
        sl = slice(hd * HEAD_DIM, (hd + 1) * HEAD_DIM)
        inter.append(_dot(ops["qa"][:, sl], states[hd].T.astype(BF16)))
    return new_states, scores, inter


def _hgrn2_outputs(scores, inter, vb, g, hnorm, masks):
    outs = []
    for pi in range(N_HEADS // 2):
        p = jnp.where(masks[0], scores[pi][0], 0.0)
        for i in range(len(LEVELS)):
            p = jnp.where(masks[1 + i], scores[pi][1 + i], p)
        o2 = _dot(p.astype(BF16), _block_diag(_head_pair(vb, pi)))
        for j in range(2):
            hd = 2 * pi + j
            sl = slice(hd * HEAD_DIM, (hd + 1) * HEAD_DIM)
            o = inter[hd] + o2[:, j * HEAD_DIM:(j + 1) * HEAD_DIM]
            o = o * lax.rsqrt(jnp.mean(o * o, axis=-1, keepdims=True) + EPS) * hnorm[:, sl]
            gh = g[:, sl]
            outs.append((o * (gh * jax.nn.sigmoid(gh))).astype(BF16))
    return jnp.concatenate(outs, axis=1)


def _gelu_tanh(x):
    c1 = math.sqrt(2.0 / math.pi)
    return x * (0.5 + 0.5 * jnp.tanh(x * (c1 + (c1 * 0.044715) * (x * x))))


def _mixer_kernel(x_ref, gmix_ref, win_hbm, lbp_ref, hnorm_ref, lng_ref, lnb_ref, wsp_ref, bsp_ref, wout_hbm,
                  tmat_ref, lvl_ref, o_ref, state_ref, mix_ref, win_ref, wout_ref, stage_in, sem_in, stage_out,
                  sem_out, *, layer):
    c = RECUR_CHUNK
    n_chunks = MIX_TILE // c

    @pl.when((pl.program_id(0) == 0) & (pl.program_id(1) == 0))
    def _():
        _load_weights([(win_hbm.at[layer], win_ref, stage_in, sem_in),
                       (wout_hbm.at[layer], wout_ref, stage_out, sem_out)])

    @pl.when(pl.program_id(1) == 0)
    def _():
        state_ref[...] = jnp.zeros(state_ref.shape, F32)

    halves = [slice(i * MIX_TILE // 2, (i + 1) * MIX_TILE // 2) for i in range(2)]
    xs = [x_ref[0, r, :] for r in halves]
    hs = [_rmsnorm(xh, gmix_ref[...]).astype(BF16) for xh in xs]
    h = jnp.concatenate(hs, axis=0)
    proj = {}

    def project(k, j, by_halves=False):
        lo = k * A_WIDTH + j * PROJ_COLS
        w = win_ref[:, lo:lo + PROJ_COLS]
        proj[k, j] = jnp.concatenate([_dot(hh, w) for hh in hs], axis=0) if by_halves else _dot(h, w)

    def projected(k, rows):
        return jnp.concatenate([proj[k, j][rows] for j in range(A_WIDTH // PROJ_COLS)], axis=1)

    lp = lbp_ref[...]
    ex = jnp.exp(lp - jnp.max(lp, axis=0, keepdims=True))
    prob = ex / jnp.sum(ex, axis=0, keepdims=True)
    lbs = jnp.sum(prob[:layer + 1, :], axis=0, keepdims=True) - prob[0:1, :]
    hnorm = hnorm_ref[...]
    tmat = tmat_ref[...]
    lvl = lvl_ref[...]
    masks = [lvl == i for i in range(1 + len(LEVELS))]

    p = SPATIAL_CHUNK
    prow = lax.broadcasted_iota(jnp.int32, (p, p), 0)
    pcol = lax.broadcasted_iota(jnp.int32, (p, p), 1)
    w_tril = [jnp.where(prow >= pcol, wsp_ref[gi], 0.0).astype(BF16) for gi in range(N_GROUPS)]
    lng = lng_ref[...]
    lnb = lnb_ref[...]
    bias = bsp_ref[...]

    def spatial_gates(si):
        rows = slice(si * p, (si + 1) * p)
        u = _gelu_tanh(projected(4, rows))
        vv = _gelu_tanh(projected(5, rows))
        mu = jnp.mean(vv, axis=-1, keepdims=True)
        var = jnp.mean(jnp.square(vv - mu), axis=-1, keepdims=True)
        return u, ((vv - mu) * lax.rsqrt(var + EPS) * lng + lnb).astype(BF16)

    def spatial_mix(si, u, vn):
        rows = slice(si * p, (si + 1) * p)
        for gi in range(N_GROUPS):
            lo, hi = gi * GROUP_DIM, (gi + 1) * GROUP_DIM
            s = _dot(w_tril[gi], vn[:, lo:hi]) + bias[:, lo:hi]
            mix_ref[rows, A_WIDTH + lo:A_WIDTH + hi] = (u[:, lo:hi] * s).astype(BF16)

    n_spatial = MIX_TILE // p
    project(5, 0, by_halves=True)
    project(5, 1)
    project(4, 0)
    project(4, 1)
    def field(ci, k):
        return projected(k, slice(ci * c, (ci + 1) * c))

    operands, products = {}, {}

    def prepare(ci):
        operands[ci] = _hgrn2_operands(field(ci, 0), field(ci, 1), lbs, tmat)

    def finish(ci):
        scores, inter, vb = products.pop(ci)
        mix_ref[ci * c:(ci + 1) * c, 0:A_WIDTH] = _hgrn2_outputs(scores, inter, vb, field(ci, 3), hnorm, masks)

    gated = []
    slots = ((1, 0), (1, 1), (0, 0), (0, 1))
    for i, (k, j) in enumerate(slots):
        project(k, j)
        for si in range(i * n_spatial // len(slots), (i + 1) * n_spatial // len(slots)):
            gated.append(spatial_gates(si))
    slots = ((2, 0), (2, 1), (3, 0), (3, 1))
    for i, (k, j) in enumerate(slots):
        project(k, j)
        for ci in range(i * PREPARE_AHEAD // len(slots), (i + 1) * PREPARE_AHEAD // len(slots)):
            prepare(ci)
    for si in range(n_spatial):
        spatial_mix(si, *gated[si])

    states = [state_ref[hd] for hd in range(N_HEADS)]
    for ci in range(n_chunks):
        vb = field(ci, 2).astype(BF16)
        states, scores, inter = _hgrn2_products(operands.pop(ci), vb, states)
        products[ci] = (scores, inter, vb)
        if ci >= 1:
            finish(ci - 1)
        if ci + PREPARE_AHEAD < n_chunks:
            prepare(ci + PREPARE_AHEAD)
    finish(n_chunks - 1)
    for hd in range(N_HEADS):
        state_ref[hd] = states[hd]

    for r, xh in zip(halves, xs):
        o_ref[0, r, :] = xh + _dot(mix_ref[r, :], wout_ref[...])


def _mixer(x, gmix, win, lbp, hnorm, lng, lnb, wsp, bsp, wout, *, layer):
    bsz, seq, _ = x.shape
    c = RECUR_CHUNK
    tile = pl.BlockSpec((1, MIX_TILE, D_MODEL), lambda b, s: (b, s, 0))
    tmat = jnp.asarray(_prefix_sum_matrix(), BF16)
    lvl = jnp.asarray(_level_ids())
    return pl.pallas_call(
        functools.partial(_mixer_kernel, layer=layer),
        grid=(bsz, seq // MIX_TILE),
        in_specs=[tile, _resident((1, D_MODEL)), _hbm(), _resident((DEPTH, A_WIDTH)),
                  _resident((1, A_WIDTH)), _resident((1, B_WIDTH)), _resident((1, B_WIDTH)),
                  _resident((N_GROUPS, SPATIAL_CHUNK, SPATIAL_CHUNK)), _resident((SPATIAL_CHUNK, B_WIDTH)),
                  _hbm(), _resident((c, N_SPLIT * c)), _resident((c, 2 * c))],
        out_specs=tile,
        out_shape=jax.ShapeDtypeStruct(x.shape, F32),
        scratch_shapes=[pltpu.VMEM((N_HEADS, HEAD_DIM, HEAD_DIM), F32),
                        pltpu.VMEM((MIX_TILE, D_MODEL), BF16),
                        pltpu.VMEM((D_MODEL, D_IN), BF16), pltpu.VMEM((D_MODEL, D_MODEL), BF16),
                        *_stage(D_MODEL, D_IN), *_stage(D_MODEL, D_MODEL)],
        compiler_params=pltpu.CompilerParams(dimension_semantics=("arbitrary", "arbitrary"),
                                             vmem_limit_bytes=VMEM_LIMIT_BYTES),
        name="mixer",
    )(x, gmix, win, lbp, hnorm, lng, lnb, wsp, bsp, wout, tmat, lvl)


def kernel(x, norm_ffn1, ffn1_w_gate, ffn1_w_up, ffn1_w_down, norm_mix, w_in, lb_param, hgrn_norm, ln_v_gain,
           ln_v_bias, w_spatial, b_spatial, w_out, norm_ffn2, ffn2_w_gate, ffn2_w_up, ffn2_w_down, norm_final):
    bsz, seq, d = x.shape
    row = lambda t: t.reshape(1, -1).astype(F32)
    lbp = lb_param.astype(F32)
    gf = row(norm_final)
    for l in range(DEPTH):
        x = _ffn(x.reshape(bsz * seq, d), row(norm_ffn1[l]), ffn1_w_gate, ffn1_w_up, ffn1_w_down, gf,
                 layer=l, final=False).reshape(bsz, seq, d)
        bsp = jnp.repeat(b_spatial[l].astype(F32).T, GROUP_DIM, axis=1)
        x = _mixer(x, row(norm_mix[l]), w_in, lbp, row(hgrn_norm[l]), row(ln_v_gain[l]), row(ln_v_bias[l]),
                   w_spatial[l].astype(F32), bsp, w_out, layer=l)
        x = _ffn(x.reshape(bsz * seq, d), row(norm_ffn2[l]), ffn2_w_gate, ffn2_w_up, ffn2_w_down, gf,
                 layer=l, final=(l == DEPTH - 1)).reshape(bsz, seq, d)
    return x
```

```python
import functools
import math

import jax
import jax.numpy as jnp
import numpy as np
from jax import lax
from jax.experimental import pallas as pl
from jax.experimental.pallas import tpu as pltpu

D_MODEL = 1024
DEPTH = 2
A_WIDTH = 512
HEAD_DIM = 128
N_HEADS = A_WIDTH // HEAD_DIM
B_WIDTH = 512
N_GROUPS = 4
GROUP_DIM = B_WIDTH // N_GROUPS
D_IN = 4 * A_WIDTH + 2 * B_WIDTH
D_FF = int(math.ceil(8 * D_MODEL / 3 / 128)) * 128
SPATIAL_CHUNK = 128
RECUR_CHUNK = 64
FFN_RES = 0.5
EPS = 1e-6
F_MIN = 1e-20

SUBLANES = 8
LEVELS = (32, 16, 8, 4, 2)
N_SPLIT = 3
FFN_TILE = 1024
FFN_ROWS = 256
STAGE_ELEMS = 3 * 64 * 1024
STAGE_SLOTS = 4
MIX_TILE = 1024
PROJ_COLS = 256
VMEM_LIMIT_BYTES = 56 * 1024 * 1024

F32 = jnp.float32
BF16 = jnp.bfloat16


def _rmsnorm(x, g):
    return x * lax.rsqrt(jnp.mean(x * x, axis=-1, keepdims=True) + EPS) * g


def _dot(a, b):
    return jnp.dot(a, b, preferred_element_type=F32)


def _dot_nt(a, b):
    return lax.dot_general(a, b, (((1,), (1,)), ((), ())), preferred_element_type=F32)


def _dot_tn(a, b):
    return lax.dot_general(a, b, (((0,), (0,)), ((), ())), preferred_element_type=F32)


def _hbm():
    return pl.BlockSpec(memory_space=pl.ANY)


def _stage(rows, cols):
    n = max(r for r in range(SUBLANES, rows + 1, SUBLANES) if rows % r == 0 and r * cols <= STAGE_ELEMS)
    return [pltpu.VMEM((STAGE_SLOTS, n, cols), F32), pltpu.SemaphoreType.DMA((STAGE_SLOTS,))]


def _load_weights(jobs):
    chunks = [(src, dst, stage, sems, r0)
              for src, dst, stage, sems in jobs for r0 in range(0, dst.shape[0], stage.shape[1])]
    ahead = STAGE_SLOTS - 1

    def copy(i):
        src, _, stage, sems, r0 = chunks[i]
        slot = i % STAGE_SLOTS
        return pltpu.make_async_copy(src.at[pl.ds(r0, stage.shape[1]), :], stage.at[slot], sems.at[slot])

    for i in range(min(ahead, len(chunks))):
        copy(i).start()
    for i, (_, dst, stage, _, r0) in enumerate(chunks):
        if i + ahead < len(chunks):
            copy(i + ahead).start()
        copy(i).wait()
        dst[r0:r0 + stage.shape[1], :] = stage[i % STAGE_SLOTS].astype(BF16)


def _ffn_kernel(x_ref, g_ref, wg_hbm, wu_hbm, wd_hbm, gf_ref, o_ref, wg_ref, wu_ref, wd_ref, stage_in, sem_in,
                stage_out, sem_out, *, layer, final):
    @pl.when(pl.program_id(0) == 0)
    def _():
        _load_weights([(wg_hbm.at[layer], wg_ref, stage_in, sem_in), (wu_hbm.at[layer], wu_ref, stage_in, sem_in),
                       (wd_hbm.at[layer], wd_ref, stage_out, sem_out)])

    n_groups = FFN_TILE // FFN_ROWS
    rows = [slice(i * FFN_ROWS, (i + 1) * FFN_ROWS) for i in range(n_groups)]

    def gate_up(i):
        h = _rmsnorm(x_ref[rows[i], :], g_ref[...]).astype(BF16)
        return _dot(h, wg_ref[...]), _dot(h, wu_ref[...])

    def down(i, gate, up):
        act = (gate * jax.nn.sigmoid(gate) * up).astype(BF16)
        out = x_ref[rows[i], :] + FFN_RES * _dot(act, wd_ref[...])
        if final:
            out = _rmsnorm(out, gf_ref[...])
        o_ref[rows[i], :] = out

    pending = gate_up(0)
    for i in range(1, n_groups):
        nxt = gate_up(i)
        down(i - 1, *pending)
        pending = nxt
    down(n_groups - 1, *pending)


def _resident(shape):
    return pl.BlockSpec(shape, lambda *_: (0,) * len(shape), pipeline_mode=pl.Buffered(1))


def _ffn(x2d, g, wg, wu, wd, gf, *, layer, final):
    m = x2d.shape[0]
    tile = pl.BlockSpec((FFN_TILE, D_MODEL), lambda i: (i, 0))
    return pl.pallas_call(
        functools.partial(_ffn_kernel, layer=layer, final=final),
        grid=(m // FFN_TILE,),
        in_specs=[tile, _resident((1, D_MODEL)), _hbm(), _hbm(), _hbm(), _resident((1, D_MODEL))],
        out_specs=tile,
        out_shape=jax.ShapeDtypeStruct(x2d.shape, F32),
        scratch_shapes=[pltpu.VMEM((D_MODEL, D_FF), BF16), pltpu.VMEM((D_MODEL, D_FF), BF16),
                        pltpu.VMEM((D_FF, D_MODEL), BF16),
                        *_stage(D_MODEL, D_FF), *_stage(D_FF, D_MODEL)],
        compiler_params=pltpu.CompilerParams(dimension_semantics=("arbitrary",),
                                             vmem_limit_bytes=VMEM_LIMIT_BYTES),
        name="ffn_final" if final else "ffn",
    )(x2d, g, wg, wu, wd, gf)


def _split3(x):
    hi = x.astype(BF16)
    r1 = x - hi.astype(F32)
    mid = r1.astype(BF16)
    lo = (r1 - mid.astype(F32)).astype(BF16)
    return jnp.concatenate([hi, mid, lo], axis=0)


def _prefix_sum_matrix():
    c = RECUR_CHUNK
    tril = (np.arange(c)[None, :] <= np.arange(c)[:, None]).astype(np.float32)
    return np.concatenate([tril] * N_SPLIT, axis=1)


def _level_ids():
    c = RECUR_CHUNK
    t = np.arange(c)[:, None]
    s = np.arange(c)[None, :]
    ids = np.full((c, c), -1, np.int32)
    ids[((t // 2) == (s // 2)) & (s <= t)] = 0
    for i, m in enumerate(LEVELS):
        pair = ((t // (2 * m)) == (s // (2 * m))) & ((t // m) % 2 == 1) & ((s // m) % 2 == 0)
        ids[pair] = 1 + i
    return np.concatenate([ids, ids], axis=1)


def _rows3(x):
    return x.reshape(RECUR_CHUNK // SUBLANES, SUBLANES, A_WIDTH)


def _odd_rows(m):
    return (lax.broadcasted_iota(jnp.int32, (1, SUBLANES, A_WIDTH), 1) // m) % 2 == 1


def _pair_operand(q, kk, x, m):
    c = RECUR_CHUNK
    if m >= SUBLANES:
        parts = [(q if b % 2 else kk)[b * m:(b + 1) * m] * x[b * m:(b + 1) * m] for b in range(c // m)]
        return jnp.concatenate(parts, axis=0)
    return (jnp.where(_odd_rows(m), _rows3(q), _rows3(kk)) * _rows3(x)).reshape(c, A_WIDTH)


def _level_decay(a, m):
    c = RECUR_CHUNK
    if m >= SUBLANES:
        parts = []
        for j in range(c // (2 * m)):
            ref = a[(2 * j + 1) * m - 1:(2 * j + 1) * m]
            parts += [ref - a[2 * j * m:(2 * j + 1) * m], a[(2 * j + 1) * m:(2 * j + 2) * m] - ref]
        return jnp.exp2(jnp.concatenate(parts, axis=0))
    a3 = _rows3(a)
    row = lax.broadcasted_iota(jnp.int32, (1, SUBLANES, A_WIDTH), 1)
    ref = a3[:, m - 1:m, :]
    for j in range(1, SUBLANES // (2 * m)):
        ref = jnp.where(row >= 2 * j * m, a3[:, (2 * j + 1) * m - 1:(2 * j + 1) * m, :], ref)
    sign = jnp.where(_odd_rows(m), 1.0, -1.0)
    return jnp.exp2((a3 - ref) * sign).reshape(c, A_WIDTH)


def _hgrn2_operands(q, z, lb, tmat):
    c = RECUR_CHUNK
    e = jnp.exp(-jnp.abs(z))
    inv = 1.0 / (1.0 + e)
    einv = e * inv
    sig_pos = jnp.where(z >= 0, inv, einv)
    sig_neg = jnp.where(z >= 0, einv, inv)
    fc = jnp.maximum(lb + (1.0 - lb) * sig_pos, F_MIN)
    kk = (1.0 - lb) * sig_neg

    a = _dot(tmat, _split3(jnp.log2(fc)))
    xa = jnp.exp2(a)
    odd = _odd_rows(1)
    return dict(
        qa=(q * xa).astype(BF16),
        kd=(kk * jnp.exp2(a[c - 1:c] - a)).astype(BF16),
        dec=xa[c - 1:c, :],
        ws=[_pair_operand(q, kk, _level_decay(a, m), m).astype(BF16) for m in LEVELS],
        q2=jnp.where(odd, _rows3(q * fc), _rows3(q)).reshape(c, A_WIDTH).astype(BF16),
        k2=jnp.where(odd, _rows3(kk / fc), _rows3(kk)).reshape(c, A_WIDTH).astype(BF16),
    )


def _head_pair(x, pi):
    return x[:, 2 * pi * HEAD_DIM:2 * (pi + 1) * HEAD_DIM]


def _block_diag(x2):
    zero = jnp.zeros((x2.shape[0], HEAD_DIM), x2.dtype)
    return jnp.concatenate([jnp.concatenate([x2[:, :HEAD_DIM], zero], axis=1),
                            jnp.concatenate([zero, x2[:, HEAD_DIM:]], axis=1)], axis=0)


def _hgrn2_products(ops, vb, states):
    new_states, scores, inter = [], [], []
    for hd in range(N_HEADS):
        sl = slice(hd * HEAD_DIM, (hd + 1) * HEAD_DIM)
        new_states.append(states[hd] * ops["dec"][:, sl] + _dot_tn(vb[:, sl], ops["kd"][:, sl]))
    for pi in range(N_HEADS // 2):
        pair_scores = [_dot_nt(_head_pair(ops["q2"], pi), _block_diag(_head_pair(ops["k2"], pi)))]
        for w in ops["ws"]:
            pair_scores.append(_dot_nt(_head_pair(w, pi), _block_diag(_head_pair(w, pi))))
        scores.append(pair_scores)
    for hd in range(N_HEADS):
        sl = slice(hd * HEAD_DIM, (hd + 1) * HEAD_DIM)
        inter.append(_dot(ops["qa"][:, sl], states[hd].T.astype(BF16)))
    return new_states, scores, inter


def _hgrn2_outputs(scores, inter, vb, g, hnorm, masks):
    outs = []
    for pi in range(N_HEADS // 2):
        p = jnp.where(masks[0], scores[pi][0], 0.0)
        for i in range(len(LEVELS)):
            p = jnp.where(masks[1 + i], scores[pi][1 + i], p)
        o2 = _dot(p.astype(BF16), _block_diag(_head_pair(vb, pi)))
        for j in range(2):
            hd = 2 * pi + j
            sl = slice(hd * HEAD_DIM, (hd + 1) * HEAD_DIM)
            o = inter[hd] + o2[:, j * HEAD_DIM:(j + 1) * HEAD_DIM]
            o = o * lax.rsqrt(jnp.mean(o * o, axis=-1, keepdims=True) + EPS) * hnorm[:, sl]
            gh = g[:, sl]
            outs.append((o * (gh * jax.nn.sigmoid(gh))).astype(BF16))
    return jnp.concatenate(outs, axis=1)


def _gelu_tanh(x):
    c1 = math.sqrt(2.0 / math.pi)
    return x * (0.5 + 0.5 * jnp.tanh(x * (c1 + (c1 * 0.044715) * (x * x))))


def _mixer_kernel(x_ref, gmix_ref, win_hbm, lbp_ref, hnorm_ref, lng_ref, lnb_ref, wsp_ref, bsp_ref, wout_hbm,
                  tmat_ref, lvl_ref, o_ref, state_ref, mix_ref, win_ref, wout_ref, stage_in, sem_in, stage_out,
                  sem_out, *, layer):
    c = RECUR_CHUNK
    n_chunks = MIX_TILE // c

    @pl.when((pl.program_id(0) == 0) & (pl.program_id(1) == 0))
    def _():
        _load_weights([(win_hbm.at[layer], win_ref, stage_in, sem_in),
                       (wout_hbm.at[layer], wout_ref, stage_out, sem_out)])

    @pl.when(pl.program_id(1) == 0)
    def _():
        state_ref[...] = jnp.zeros(state_ref.shape, F32)

    halves = [slice(i * MIX_TILE // 2, (i + 1) * MIX_TILE // 2) for i in range(2)]
    xs = [x_ref[0, r, :] for r in halves]
    hs = [_rmsnorm(xh, gmix_ref[...]).astype(BF16) for xh in xs]
    h = jnp.concatenate(hs, axis=0)
    proj = {}

    def project(k, j, by_halves=False):
        lo = k * A_WIDTH + j * PROJ_COLS
        w = win_ref[:, lo:lo + PROJ_COLS]
        proj[k, j] = jnp.concatenate([_dot(hh, w) for hh in hs], axis=0) if by_halves else _dot(h, w)

    def projected(k, rows):
        return jnp.concatenate([proj[k, j][rows] for j in range(A_WIDTH // PROJ_COLS)], axis=1)

    lp = lbp_ref[...]
    ex = jnp.exp(lp - jnp.max(lp, axis=0, keepdims=True))
    prob = ex / jnp.sum(ex, axis=0, keepdims=True)
    lbs = jnp.sum(prob[:layer + 1, :], axis=0, keepdims=True) - prob[0:1, :]
    hnorm = hnorm_ref[...]
    tmat = tmat_ref[...]
    lvl = lvl_ref[...]
    masks = [lvl == i for i in range(1 + len(LEVELS))]

    p = SPATIAL_CHUNK
    prow = lax.broadcasted_iota(jnp.int32, (p, p), 0)
    pcol = lax.broadcasted_iota(jnp.int32, (p, p), 1)
    w_tril = [jnp.where(prow >= pcol, wsp_ref[gi], 0.0).astype(BF16) for gi in range(N_GROUPS)]
    lng = lng_ref[...]
    lnb = lnb_ref[...]
    bias = bsp_ref[...]

    def spatial_gates(si):
        rows = slice(si * p, (si + 1) * p)
        u = _gelu_tanh(projected(4, rows))
        vv = _gelu_tanh(projected(5, rows))
        mu = jnp.mean(vv, axis=-1, keepdims=True)
        var = jnp.mean(jnp.square(vv - mu), axis=-1, keepdims=True)
        return u, ((vv - mu) * lax.rsqrt(var + EPS) * lng + lnb).astype(BF16)

    def spatial_mix(si, u, vn):
        rows = slice(si * p, (si + 1) * p)
        for gi in range(N_GROUPS):
            lo, hi = gi * GROUP_DIM, (gi + 1) * GROUP_DIM
            s = _dot(w_tril[gi], vn[:, lo:hi]) + bias[:, lo:hi]
            mix_ref[rows, A_WIDTH + lo:A_WIDTH + hi] = (u[:, lo:hi] * s).astype(BF16)

    def field(ci, k):
        return projected(k, slice(ci * c, (ci + 1) * c))

    operands, products = {}, {}

    def prepare(ci):
        operands[ci] = _hgrn2_operands(field(ci, 0), field(ci, 1), lbs, tmat)

    def finish(ci):
        scores, inter, vb = products.pop(ci)
        mix_ref[ci * c:(ci + 1) * c, 0:A_WIDTH] = _hgrn2_outputs(scores, inter, vb, field(ci, 3), hnorm, masks)

    project(1, 0, by_halves=True)
    project(1, 1)
    project(0, 0)
    project(0, 1)
    slots = ((5, 0), (5, 1), (4, 0), (4, 1), (2, 0), (2, 1), (3, 0), (3, 1))
    for i, (k, j) in enumerate(slots):
        project(k, j)
        for ci in range(i * n_chunks // len(slots), (i + 1) * n_chunks // len(slots)):
            prepare(ci)

    states = [state_ref[hd] for hd in range(N_HEADS)]
    for ci in range(n_chunks):
        vb = field(ci, 2).astype(BF16)
        states, scores, inter = _hgrn2_products(operands.pop(ci), vb, states)
        products[ci] = (scores, inter, vb)
        if ci >= 1:
            finish(ci - 1)
        if ci % (p // c) == 0:
            spatial_mix(ci // (p // c), *spatial_gates(ci // (p // c)))
    finish(n_chunks - 1)
    for hd in range(N_HEADS):
        state_ref[hd] = states[hd]

    for r, xh in zip(halves, xs):
        o_ref[0, r, :] = xh + _dot(mix_ref[r, :], wout_ref[...])


def _mixer(x, gmix, win, lbp, hnorm, lng, lnb, wsp, bsp, wout, *, layer):
    bsz, seq, _ = x.shape
    c = RECUR_CHUNK
    tile = pl.BlockSpec((1, MIX_TILE, D_MODEL), lambda b, s: (b, s, 0))
    tmat = jnp.asarray(_prefix_sum_matrix(), BF16)
    lvl = jnp.asarray(_level_ids())
    return pl.pallas_call(
        functools.partial(_mixer_kernel, layer=layer),
        grid=(bsz, seq // MIX_TILE),
        in_specs=[tile, _resident((1, D_MODEL)), _hbm(), _resident((DEPTH, A_WIDTH)),
                  _resident((1, A_WIDTH)), _resident((1, B_WIDTH)), _resident((1, B_WIDTH)),
                  _resident((N_GROUPS, SPATIAL_CHUNK, SPATIAL_CHUNK)), _resident((SPATIAL_CHUNK, B_WIDTH)),
                  _hbm(), _resident((c, N_SPLIT * c)), _resident((c, 2 * c))],
        out_specs=tile,
        out_shape=jax.ShapeDtypeStruct(x.shape, F32),
        scratch_shapes=[pltpu.VMEM((N_HEADS, HEAD_DIM, HEAD_DIM), F32),
                        pltpu.VMEM((MIX_TILE, D_MODEL), BF16),
                        pltpu.VMEM((D_MODEL, D_IN), BF16), pltpu.VMEM((D_MODEL, D_MODEL), BF16),
                        *_stage(D_MODEL, D_IN), *_stage(D_MODEL, D_MODEL)],
        compiler_params=pltpu.CompilerParams(dimension_semantics=("arbitrary", "arbitrary"),
                                             vmem_limit_bytes=VMEM_LIMIT_BYTES),
        name="mixer",
    )(x, gmix, win, lbp, hnorm, lng, lnb, wsp, bsp, wout, tmat, lvl)


def kernel(x, norm_ffn1, ffn1_w_gate, ffn1_w_up, ffn1_w_down, norm_mix, w_in, lb_param, hgrn_norm, ln_v_gain,
           ln_v_bias, w_spatial, b_spatial, w_out, norm_ffn2, ffn2_w_gate, ffn2_w_up, ffn2_w_down, norm_final):
    bsz, seq, d = x.shape
    row = lambda t: t.reshape(1, -1).astype(F32)
    lbp = lb_param.astype(F32)
    gf = row(norm_final)
    for l in range(DEPTH):
        x = _ffn(x.reshape(bsz * seq, d), row(norm_ffn1[l]), ffn1_w_gate, ffn1_w_up, ffn1_w_down, gf,
                 layer=l, final=False).reshape(bsz, seq, d)
        bsp = jnp.repeat(b_spatial[l].astype(F32).T, GROUP_DIM, axis=1)
        x = _mixer(x, row(norm_mix[l]), w_in, lbp, row(hgrn_norm[l]), row(ln_v_gain[l]), row(ln_v_bias[l]),
                   w_spatial[l].astype(F32), bsp, w_out, layer=l)
        x = _ffn(x.reshape(bsz * seq, d), row(norm_ffn2[l]), ffn2_w_gate, ffn2_w_up, ffn2_w_down, gf,
                 layer=l, final=(l == DEPTH - 1)).reshape(bsz, seq, d)
    return x
```

```python
import functools
import math

import jax
import jax.numpy as jnp
import numpy as np
from jax import lax
from jax.experimental import pallas as pl
from jax.experimental.pallas import tpu as pltpu

D_MODEL = 1024
DEPTH = 2
A_WIDTH = 512
HEAD_DIM = 128
N_HEADS = A_WIDTH // HEAD_DIM
B_WIDTH = 512
N_GROUPS = 4
GROUP_DIM = B_WIDTH // N_GROUPS
D_IN = 4 * A_WIDTH + 2 * B_WIDTH
D_FF = int(math.ceil(8 * D_MODEL / 3 / 128)) * 128
SPATIAL_CHUNK = 128
RECUR_CHUNK = 64
FFN_RES = 0.5
EPS = 1e-6
F_MIN = 1e-20

SUBLANES = 8
LEVELS = (32, 16, 8, 4, 2)
N_SPLIT = 3
FFN_TILE = 1024
FFN_ROWS = 256
STAGE_ELEMS = 3 * 64 * 1024
FFN_STAGE_SLOTS = 8
MIX_STAGE_SLOTS = 6
MIX_TILE = 1024
PROJ_COLS = 256
VMEM_LIMIT_BYTES = 56 * 1024 * 1024

F32 = jnp.float32
BF16 = jnp.bfloat16


def _rmsnorm(x, g):
    return x * lax.rsqrt(jnp.mean(x * x, axis=-1, keepdims=True) + EPS) * g


def _dot(a, b):
    return jnp.dot(a, b, preferred_element_type=F32)


def _dot_nt(a, b):
    return lax.dot_general(a, b, (((1,), (1,)), ((), ())), preferred_element_type=F32)


def _dot_tn(a, b):
    return lax.dot_general(a, b, (((0,), (0,)), ((), ())), preferred_element_type=F32)


def _hbm():
    return pl.BlockSpec(memory_space=pl.ANY)


def _stage(rows, cols, slots):
    n = max(r for r in range(SUBLANES, rows + 1, SUBLANES) if rows % r == 0 and r * cols <= STAGE_ELEMS)
    return [pltpu.VMEM((slots, n, cols), F32), pltpu.SemaphoreType.DMA((slots,))]


def _load_weights(jobs):
    chunks = [(src, dst, stage, sems, r0)
              for src, dst, stage, sems in jobs for r0 in range(0, dst.shape[0], stage.shape[1])]
    slots = jobs[0][2].shape[0]
    assert all(stage.shape[0] == slots for _, _, stage, _ in jobs)
    ahead = slots - 1

    def copy(i):
        src, _, stage, sems, r0 = chunks[i]
        slot = i % slots
        return pltpu.make_async_copy(src.at[pl.ds(r0, stage.shape[1]), :], stage.at[slot], sems.at[slot])

    for i in range(min(ahead, len(chunks))):
        copy(i).start()
    for i, (_, dst, stage, _, r0) in enumerate(chunks):
        if i + ahead < len(chunks):
            copy(i + ahead).start()
        copy(i).wait()
        dst[r0:r0 + stage.shape[1], :] = stage[i % slots].astype(BF16)


def _ffn_kernel(x_ref, g_ref, wg_hbm, wu_hbm, wd_hbm, gf_ref, o_ref, wg_ref, wu_ref, wd_ref, stage_in, sem_in,
                stage_out, sem_out, *, layer, final):
    @pl.when(pl.program_id(0) == 0)
    def _():
        _load_weights([(wg_hbm.at[layer], wg_ref, stage_in, sem_in), (wu_hbm.at[layer], wu_ref, stage_in, sem_in),
                       (wd_hbm.at[layer], wd_ref, stage_out, sem_out)])

    n_groups = FFN_TILE // FFN_ROWS
    rows = [slice(i * FFN_ROWS, (i + 1) * FFN_ROWS) for i in range(n_groups)]

    def gate_up(i):
        h = _rmsnorm(x_ref[rows[i], :], g_ref[...]).astype(BF16)
        return _dot(h, wg_ref[...]), _dot(h, wu_ref[...])

    def down(i, gate, up):
        act = (gate * jax.nn.sigmoid(gate) * up).astype(BF16)
        out = x_ref[rows[i], :] + FFN_RES * _dot(act, wd_ref[...])
        if final:
            out = _rmsnorm(out, gf_ref[...])
        o_ref[rows[i], :] = out

    pending = gate_up(0)
    for i in range(1, n_groups):
        nxt = gate_up(i)
        down(i - 1, *pending)
        pending = nxt
    down(n_groups - 1, *pending)


def _resident(shape):
    return pl.BlockSpec(shape, lambda *_: (0,) * len(shape), pipeline_mode=pl.Buffered(1))


def _ffn(x2d, g, wg, wu, wd, gf, *, layer, final):
    m = x2d.shape[0]
    tile = pl.BlockSpec((FFN_TILE, D_MODEL), lambda i: (i, 0))
    return pl.pallas_call(
        functools.partial(_ffn_kernel, layer=layer, final=final),
        grid=(m // FFN_TILE,),
        in_specs=[tile, _resident((1, D_MODEL)), _hbm(), _hbm(), _hbm(), _resident((1, D_MODEL))],
        out_specs=tile,
        out_shape=jax.ShapeDtypeStruct(x2d.shape, F32),
        scratch_shapes=[pltpu.VMEM((D_MODEL, D_FF), BF16), pltpu.VMEM((D_MODEL, D_FF), BF16),
                        pltpu.VMEM((D_FF, D_MODEL), BF16),
                        *_stage(D_MODEL, D_FF, FFN_STAGE_SLOTS), *_stage(D_FF, D_MODEL, FFN_STAGE_SLOTS)],
        compiler_params=pltpu.CompilerParams(dimension_semantics=("arbitrary",),
                                             vmem_limit_bytes=VMEM_LIMIT_BYTES),
        name="ffn_final" if final else "ffn",
    )(x2d, g, wg, wu, wd, gf)


def _split3(x):
    hi = x.astype(BF16)
    r1 = x - hi.astype(F32)
    mid = r1.astype(BF16)
    lo = (r1 - mid.astype(F32)).astype(BF16)
    return jnp.concatenate([hi, mid, lo], axis=0)


def _prefix_sum_matrix():
    c = RECUR_CHUNK
    tril = (np.arange(c)[None, :] <= np.arange(c)[:, None]).astype(np.float32)
    return np.concatenate([tril] * N_SPLIT, axis=1)


def _level_ids():
    c = RECUR_CHUNK
    t = np.arange(c)[:, None]
    s = np.arange(c)[None, :]
    ids = np.full((c, c), -1, np.int32)
    ids[((t // 2) == (s // 2)) & (s <= t)] = 0
    for i, m in enumerate(LEVELS):
        pair = ((t // (2 * m)) == (s // (2 * m))) & ((t // m) % 2 == 1) & ((s // m) % 2 == 0)
        ids[pair] = 1 + i
    return np.concatenate([ids, ids], axis=1)


def _rows3(x):
    return x.reshape(RECUR_CHUNK // SUBLANES, SUBLANES, A_WIDTH)


def _odd_rows(m):
    return (lax.broadcasted_iota(jnp.int32, (1, SUBLANES, A_WIDTH), 1) // m) % 2 == 1


def _pair_operand(q, kk, x, m):
    c = RECUR_CHUNK
    if m >= SUBLANES:
        parts = [(q if b % 2 else kk)[b * m:(b + 1) * m] * x[b * m:(b + 1) * m] for b in range(c // m)]
        return jnp.concatenate(parts, axis=0)
    return (jnp.where(_odd_rows(m), _rows3(q), _rows3(kk)) * _rows3(x)).reshape(c, A_WIDTH)


def _level_decay(a, m):
    c = RECUR_CHUNK
    if m >= SUBLANES:
        parts = []
        for j in range(c // (2 * m)):
            ref = a[(2 * j + 1) * m - 1:(2 * j + 1) * m]
            parts += [ref - a[2 * j * m:(2 * j + 1) * m], a[(2 * j + 1) * m:(2 * j + 2) * m] - ref]
        return jnp.exp2(jnp.concatenate(parts, axis=0))
    a3 = _rows3(a)
    row = lax.broadcasted_iota(jnp.int32, (1, SUBLANES, A_WIDTH), 1)
    ref = a3[:, m - 1:m, :]
    for j in range(1, SUBLANES // (2 * m)):
        ref = jnp.where(row >= 2 * j * m, a3[:, (2 * j + 1) * m - 1:(2 * j + 1) * m, :], ref)
    sign = jnp.where(_odd_rows(m), 1.0, -1.0)
    return jnp.exp2((a3 - ref) * sign).reshape(c, A_WIDTH)


def _hgrn2_operands(q, z, lb, tmat):
    c = RECUR_CHUNK
    e = jnp.exp(-jnp.abs(z))
    inv = 1.0 / (1.0 + e)
    einv = e * inv
    sig_pos = jnp.where(z >= 0, inv, einv)
    sig_neg = jnp.where(z >= 0, einv, inv)
    fc = jnp.maximum(lb + (1.0 - lb) * sig_pos, F_MIN)
    kk = (1.0 - lb) * sig_neg

    a = _dot(tmat, _split3(jnp.log2(fc)))
    xa = jnp.exp2(a)
    odd = _odd_rows(1)
    return dict(
        qa=(q * xa).astype(BF16),
        kd=(kk * jnp.exp2(a[c - 1:c] - a)).astype(BF16),
        dec=xa[c - 1:c, :],
        ws=[_pair_operand(q, kk, _level_decay(a, m), m).astype(BF16) for m in LEVELS],
        q2=jnp.where(odd, _rows3(q * fc), _rows3(q)).reshape(c, A_WIDTH).astype(BF16),
        k2=jnp.where(odd, _rows3(kk / fc), _rows3(kk)).reshape(c, A_WIDTH).astype(BF16),
    )


def _head_pair(x, pi):
    return x[:, 2 * pi * HEAD_DIM:2 * (pi + 1) * HEAD_DIM]


def _block_diag(x2):
    zero = jnp.zeros((x2.shape[0], HEAD_DIM), x2.dtype)
    return jnp.concatenate([jnp.concatenate([x2[:, :HEAD_DIM], zero], axis=1),
                            jnp.concatenate([zero, x2[:, HEAD_DIM:]], axis=1)], axis=0)


def _hgrn2_products(ops, vb, states):
    new_states, scores, inter = [], [], []
    for hd in range(N_HEADS):
        sl = slice(hd * HEAD_DIM, (hd + 1) * HEAD_DIM)
        new_states.append(states[hd] * ops["dec"][:, sl] + _dot_tn(vb[:, sl], ops["kd"][:, sl]))
    for pi in range(N_HEADS // 2):
        pair_scores = [_dot_nt(_head_pair(ops["q2"], pi), _block_diag(_head_pair(ops["k2"], pi)))]
        for w in ops["ws"]:
            pair_scores.append(_dot_nt(_head_pair(w, pi), _block_diag(_head_pair(w, pi))))
        scores.append(pair_scores)
    for hd in range(N_HEADS):
        sl = slice(hd * HEAD_DIM, (hd + 1) * HEAD_DIM)
        inter.append(_dot(ops["qa"][:, sl], states[hd].T.astype(BF16)))
    return new_states, scores, inter


def _hgrn2_outputs(scores, inter, vb, g, hnorm, masks):
    outs = []
    for pi in range(N_HEADS // 2):
        p = jnp.where(masks[0], scores[pi][0], 0.0)
        for i in range(len(LEVELS)):
            p = jnp.where(masks[1 + i], scores[pi][1 + i], p)
        o2 = _dot(p.astype(BF16), _block_diag(_head_pair(vb, pi)))
        for j in range(2):
            hd = 2 * pi + j
            sl = slice(hd * HEAD_DIM, (hd + 1) * HEAD_DIM)
            o = inter[hd] + o2[:, j * HEAD_DIM:(j + 1) * HEAD_DIM]
            o = o * lax.rsqrt(jnp.mean(o * o, axis=-1, keepdims=True) + EPS) * hnorm[:, sl]
            gh = g[:, sl]
            outs.append((o * (gh * jax.nn.sigmoid(gh))).astype(BF16))
    return jnp.concatenate(outs, axis=1)


def _gelu_tanh(x):
    c1 = math.sqrt(2.0 / math.pi)
    return x * (0.5 + 0.5 * jnp.tanh(x * (c1 + (c1 * 0.044715) * (x * x))))


def _mixer_kernel(x_ref, gmix_ref, win_hbm, lbp_ref, hnorm_ref, lng_ref, lnb_ref, wsp_ref, bsp_ref, wout_hbm,
                  tmat_ref, lvl_ref, o_ref, state_ref, mix_ref, win_ref, wout_ref, stage_in, sem_in, stage_out,
                  sem_out, *, layer):
    c = RECUR_CHUNK
    n_chunks = MIX_TILE // c

    @pl.when((pl.program_id(0) == 0) & (pl.program_id(1) == 0))
    def _():
        _load_weights([(win_hbm.at[layer], win_ref, stage_in, sem_in),
                       (wout_hbm.at[layer], wout_ref, stage_out, sem_out)])

    @pl.when(pl.program_id(1) == 0)
    def _():
        state_ref[...] = jnp.zeros(state_ref.shape, F32)

    halves = [slice(i * MIX_TILE // 2, (i + 1) * MIX_TILE // 2) for i in range(2)]
    xs = [x_ref[0, r, :] for r in halves]
    hs = [_rmsnorm(xh, gmix_ref[...]).astype(BF16) for xh in xs]
    h = jnp.concatenate(hs, axis=0)
    proj = {}

    def project(k, j, by_halves=False):
        lo = k * A_WIDTH + j * PROJ_COLS
        w = win_ref[:, lo:lo + PROJ_COLS]
        proj[k, j] = jnp.concatenate([_dot(hh, w) for hh in hs], axis=0) if by_halves else _dot(h, w)

    def projected(k, rows):
        return jnp.concatenate([proj[k, j][rows] for j in range(A_WIDTH // PROJ_COLS)], axis=1)

    lp = lbp_ref[...]
    ex = jnp.exp(lp - jnp.max(lp, axis=0, keepdims=True))
    prob = ex / jnp.sum(ex, axis=0, keepdims=True)
    lbs = jnp.sum(prob[:layer + 1, :], axis=0, keepdims=True) - prob[0:1, :]
    hnorm = hnorm_ref[...]
    tmat = tmat_ref[...]
    lvl = lvl_ref[...]
    masks = [lvl == i for i in range(1 + len(LEVELS))]

    p = SPATIAL_CHUNK
    prow = lax.broadcasted_iota(jnp.int32, (p, p), 0)
    pcol = lax.broadcasted_iota(jnp.int32, (p, p), 1)
    w_tril = [jnp.where(prow >= pcol, wsp_ref[gi], 0.0).astype(BF16) for gi in range(N_GROUPS)]
    lng = lng_ref[...]
    lnb = lnb_ref[...]
    bias = bsp_ref[...]

    def spatial_gates(si):
        rows = slice(si * p, (si + 1) * p)
        u = _gelu_tanh(projected(4, rows))
        vv = _gelu_tanh(projected(5, rows))
        mu = jnp.mean(vv, axis=-1, keepdims=True)
        var = jnp.mean(jnp.square(vv - mu), axis=-1, keepdims=True)
        return u, ((vv - mu) * lax.rsqrt(var + EPS) * lng + lnb).astype(BF16)

    def spatial_mix(si, u, vn):
        rows = slice(si * p, (si + 1) * p)
        for gi in range(N_GROUPS):
            lo, hi = gi * GROUP_DIM, (gi + 1) * GROUP_DIM
            s = _dot(w_tril[gi], vn[:, lo:hi]) + bias[:, lo:hi]
            mix_ref[rows, A_WIDTH + lo:A_WIDTH + hi] = (u[:, lo:hi] * s).astype(BF16)

    def field(ci, k):
        return projected(k, slice(ci * c, (ci + 1) * c))

    operands, products = {}, {}

    def prepare(ci):
        operands[ci] = _hgrn2_operands(field(ci, 0), field(ci, 1), lbs, tmat)

    def finish(ci):
        scores, inter, vb = products.pop(ci)
        mix_ref[ci * c:(ci + 1) * c, 0:A_WIDTH] = _hgrn2_outputs(scores, inter, vb, field(ci, 3), hnorm, masks)

    project(1, 0, by_halves=True)
    project(1, 1)
    project(0, 0)
    project(0, 1)
    slots = ((5, 0), (5, 1), (4, 0), (4, 1), (2, 0), (2, 1), (3, 0), (3, 1))
    for i, (k, j) in enumerate(slots):
        project(k, j)
        for ci in range(i * n_chunks // len(slots), (i + 1) * n_chunks // len(slots)):
            prepare(ci)

    states = [state_ref[hd] for hd in range(N_HEADS)]
    for ci in range(n_chunks):
        vb = field(ci, 2).astype(BF16)
        states, scores, inter = _hgrn2_products(operands.pop(ci), vb, states)
        products[ci] = (scores, inter, vb)
        if ci >= 1:
            finish(ci - 1)
        if ci % (p // c) == 0:
            spatial_mix(ci // (p // c), *spatial_gates(ci // (p // c)))
    finish(n_chunks - 1)
    for hd in range(N_HEADS):
        state_ref[hd] = states[hd]

    for r, xh in zip(halves, xs):
        o_ref[0, r, :] = xh + _dot(mix_ref[r, :], wout_ref[...])


def _mixer(x, gmix, win, lbp, hnorm, lng, lnb, wsp, bsp, wout, *, layer):
    bsz, seq, _ = x.shape
    c = RECUR_CHUNK
    tile = pl.BlockSpec((1, MIX_TILE, D_MODEL), lambda b, s: (b, s, 0))
    tmat = jnp.asarray(_prefix_sum_matrix(), BF16)
    lvl = jnp.asarray(_level_ids())
    return pl.pallas_call(
        functools.partial(_mixer_kernel, layer=layer),
        grid=(bsz, seq // MIX_TILE),
        in_specs=[tile, _resident((1, D_MODEL)), _hbm(), _resident((DEPTH, A_WIDTH)),
                  _resident((1, A_WIDTH)), _resident((1, B_WIDTH)), _resident((1, B_WIDTH)),
                  _resident((N_GROUPS, SPATIAL_CHUNK, SPATIAL_CHUNK)), _resident((SPATIAL_CHUNK, B_WIDTH)),
                  _hbm(), _resident((c, N_SPLIT * c)), _resident((c, 2 * c))],
        out_specs=tile,
        out_shape=jax.ShapeDtypeStruct(x.shape, F32),
        scratch_shapes=[pltpu.VMEM((N_HEADS, HEAD_DIM, HEAD_DIM), F32),
                        pltpu.VMEM((MIX_TILE, D_MODEL), BF16),
                        pltpu.VMEM((D_MODEL, D_IN), BF16), pltpu.VMEM((D_MODEL, D_MODEL), BF16),
                        *_stage(D_MODEL, D_IN, MIX_STAGE_SLOTS), *_stage(D_MODEL, D_MODEL, MIX_STAGE_SLOTS)],
        compiler_params=pltpu.CompilerParams(dimension_semantics=("arbitrary", "arbitrary"),
                                             vmem_limit_bytes=VMEM_LIMIT_BYTES),
        name="mixer",
    )(x, gmix, win, lbp, hnorm, lng, lnb, wsp, bsp, wout, tmat, lvl)


def kernel(x, norm_ffn1, ffn1_w_gate, ffn1_w_up, ffn1_w_down, norm_mix, w_in, lb_param, hgrn_norm, ln_v_gain,
           ln_v_bias, w_spatial, b_spatial, w_out, norm_ffn2, ffn2_w_gate, ffn2_w_up, ffn2_w_down, norm_final):
    bsz, seq, d = x.shape
    row = lambda t: t.reshape(1, -1).astype(F32)
    lbp = lb_param.astype(F32)
    gf = row(norm_final)
    for l in range(DEPTH):
        x = _ffn(x.reshape(bsz * seq, d), row(norm_ffn1[l]), ffn1_w_gate, ffn1_w_up, ffn1_w_down, gf,
                 layer=l, final=False).reshape(bsz, seq, d)
        bsp = jnp.repeat(b_spatial[l].astype(F32).T, GROUP_DIM, axis=1)
        x = _mixer(x, row(norm_mix[l]), w_in, lbp, row(hgrn_norm[l]), row(ln_v_gain[l]), row(ln_v_bias[l]),
                   w_spatial[l].astype(F32), bsp, w_out, layer=l)
        x = _ffn(x.reshape(bsz * seq, d), row(norm_ffn2[l]), ffn2_w_gate, ffn2_w_up, ffn2_w_down, gf,
                 layer=l, final=(l == DEPTH - 1)).reshape(bsz, seq, d)
    return x
```

```python
import functools
import math

import jax
import jax.numpy as jnp
import numpy as np
from jax import lax
from jax.experimental import pallas as pl
from jax.experimental.pallas import tpu as pltpu

D_MODEL = 1024
DEPTH = 2
A_WIDTH = 512
HEAD_DIM = 128
N_HEADS = A_WIDTH // HEAD_DIM
B_WIDTH = 512
N_GROUPS = 4
GROUP_DIM = B_WIDTH // N_GROUPS
D_IN = 4 * A_WIDTH + 2 * B_WIDTH
D_FF = int(math.ceil(8 * D_MODEL / 3 / 128)) * 128
SPATIAL_CHUNK = 128
RECUR_CHUNK = 64
FFN_RES = 0.5
EPS = 1e-6
F_MIN = 1e-20

SUBLANES = 8
LEVELS = (32, 16, 8, 4, 2)
N_SPLIT = 3
FFN_TILE = 1024
FFN_ROWS = 256
STAGE_ELEMS = 3 * 64 * 1024
FFN_STAGE_SLOTS = 8
MIX_STAGE_SLOTS = 6
MIX_TILE = 1024
PROJ_COLS = 256
VMEM_LIMIT_BYTES = 56 * 1024 * 1024

F32 = jnp.float32
BF16 = jnp.bfloat16


def _rmsnorm(x, g):
    return x * lax.rsqrt(jnp.mean(x * x, axis=-1, keepdims=True) + EPS) * g


def _dot(a, b):
    return jnp.dot(a, b, preferred_element_type=F32)


def _dot_nt(a, b):
    return lax.dot_general(a, b, (((1,), (1,)), ((), ())), preferred_element_type=F32)


def _dot_tn(a, b):
    return lax.dot_general(a, b, (((0,), (0,)), ((), ())), preferred_element_type=F32)


def _hbm():
    return pl.BlockSpec(memory_space=pl.ANY)


def _stage(rows, cols, slots):
    n = max(r for r in range(SUBLANES, rows + 1, SUBLANES) if rows % r == 0 and r * cols <= STAGE_ELEMS)
    return [pltpu.VMEM((slots, n, cols), F32), pltpu.SemaphoreType.DMA((slots,))]


def _load_weights(jobs):
    chunks = [(src, dst, stage, sems, r0)
              for src, dst, stage, sems in jobs for r0 in range(0, dst.shape[0], stage.shape[1])]
    slots = jobs[0][2].shape[0]
    assert all(stage.shape[0] == slots for _, _, stage, _ in jobs)
    ahead = slots - 1

    def copy(i):
        src, _, stage, sems, r0 = chunks[i]
        slot = i % slots
        return pltpu.make_async_copy(src.at[pl.ds(r0, stage.shape[1]), :], stage.at[slot], sems.at[slot])

    for i in range(min(ahead, len(chunks))):
        copy(i).start(priority=i % 2)
    for i, (_, dst, stage, _, r0) in enumerate(chunks):
        if i + ahead < len(chunks):
            copy(i + ahead).start(priority=(i + ahead) % 2)
        copy(i).wait()
        dst[r0:r0 + stage.shape[1], :] = stage[i % slots].astype(BF16)


def _ffn_kernel(x_ref, g_ref, wg_hbm, wu_hbm, wd_hbm, gf_ref, o_ref, wg_ref, wu_ref, wd_ref, stage_in, sem_in,
                stage_out, sem_out, *, layer, final):
    @pl.when(pl.program_id(0) == 0)
    def _():
        _load_weights([(wg_hbm.at[layer], wg_ref, stage_in, sem_in), (wu_hbm.at[layer], wu_ref, stage_in, sem_in),
                       (wd_hbm.at[layer], wd_ref, stage_out, sem_out)])

    n_groups = FFN_TILE // FFN_ROWS
    rows = [slice(i * FFN_ROWS, (i + 1) * FFN_ROWS) for i in range(n_groups)]

    def gate_up(i):
        h = _rmsnorm(x_ref[rows[i], :], g_ref[...]).astype(BF16)
        return _dot(h, wg_ref[...]), _dot(h, wu_ref[...])

    def down(i, gate, up):
        act = (gate * jax.nn.sigmoid(gate) * up).astype(BF16)
        out = x_ref[rows[i], :] + FFN_RES * _dot(act, wd_ref[...])
        if final:
            out = _rmsnorm(out, gf_ref[...])
        o_ref[rows[i], :] = out

    pending = gate_up(0)
    for i in range(1, n_groups):
        nxt = gate_up(i)
        down(i - 1, *pending)
        pending = nxt
    down(n_groups - 1, *pending)


def _resident(shape):
    return pl.BlockSpec(shape, lambda *_: (0,) * len(shape), pipeline_mode=pl.Buffered(1))


def _ffn(x2d, g, wg, wu, wd, gf, *, layer, final):
    m = x2d.shape[0]
    tile = pl.BlockSpec((FFN_TILE, D_MODEL), lambda i: (i, 0))
    return pl.pallas_call(
        functools.partial(_ffn_kernel, layer=layer, final=final),
        grid=(m // FFN_TILE,),
        in_specs=[tile, _resident((1, D_MODEL)), _hbm(), _hbm(), _hbm(), _resident((1, D_MODEL))],
        out_specs=tile,
        out_shape=jax.ShapeDtypeStruct(x2d.shape, F32),
        scratch_shapes=[pltpu.VMEM((D_MODEL, D_FF), BF16), pltpu.VMEM((D_MODEL, D_FF), BF16),
                        pltpu.VMEM((D_FF, D_MODEL), BF16),
                        *_stage(D_MODEL, D_FF, FFN_STAGE_SLOTS), *_stage(D_FF, D_MODEL, FFN_STAGE_SLOTS)],
        compiler_params=pltpu.CompilerParams(dimension_semantics=("arbitrary",),
                                             vmem_limit_bytes=VMEM_LIMIT_BYTES),
        name="ffn_final" if final else "ffn",
    )(x2d, g, wg, wu, wd, gf)


def _split3(x):
    hi = x.astype(BF16)
    r1 = x - hi.astype(F32)
    mid = r1.astype(BF16)
    lo = (r1 - mid.astype(F32)).astype(BF16)
    return jnp.concatenate([hi, mid, lo], axis=0)


def _prefix_sum_matrix():
    c = RECUR_CHUNK
    tril = (np.arange(c)[None, :] <= np.arange(c)[:, None]).astype(np.float32)
    return np.concatenate([tril] * N_SPLIT, axis=1)


def _level_ids():
    c = RECUR_CHUNK
    t = np.arange(c)[:, None]
    s = np.arange(c)[None, :]
    ids = np.full((c, c), -1, np.int32)
    ids[((t // 2) == (s // 2)) & (s <= t)] = 0
    for i, m in enumerate(LEVELS):
        pair = ((t // (2 * m)) == (s // (2 * m))) & ((t // m) % 2 == 1) & ((s // m) % 2 == 0)
        ids[pair] = 1 + i
    return np.concatenate([ids, ids], axis=1)


def _rows3(x):
    return x.reshape(RECUR_CHUNK // SUBLANES, SUBLANES, A_WIDTH)


def _odd_rows(m):
    return (lax.broadcasted_iota(jnp.int32, (1, SUBLANES, A_WIDTH), 1) // m) % 2 == 1


def _pair_operand(q, kk, x, m):
    c = RECUR_CHUNK
    if m >= SUBLANES:
        parts = [(q if b % 2 else kk)[b * m:(b + 1) * m] * x[b * m:(b + 1) * m] for b in range(c // m)]
        return jnp.concatenate(parts, axis=0)
    return (jnp.where(_odd_rows(m), _rows3(q), _rows3(kk)) * _rows3(x)).reshape(c, A_WIDTH)


def _level_decay(a, m):
    c = RECUR_CHUNK
    if m >= SUBLANES:
        parts = []
        for j in range(c // (2 * m)):
            ref = a[(2 * j + 1) * m - 1:(2 * j + 1) * m]
            parts += [ref - a[2 * j * m:(2 * j + 1) * m], a[(2 * j + 1) * m:(2 * j + 2) * m] - ref]
        return jnp.exp2(jnp.concatenate(parts, axis=0))
    a3 = _rows3(a)
    row = lax.broadcasted_iota(jnp.int32, (1, SUBLANES, A_WIDTH), 1)
    ref = a3[:, m - 1:m, :]
    for j in range(1, SUBLANES // (2 * m)):
        ref = jnp.where(row >= 2 * j * m, a3[:, (2 * j + 1) * m - 1:(2 * j + 1) * m, :], ref)
    sign = jnp.where(_odd_rows(m), 1.0, -1.0)
    return jnp.exp2((a3 - ref) * sign).reshape(c, A_WIDTH)


def _hgrn2_operands(q, z, lb, tmat):
    c = RECUR_CHUNK
    e = jnp.exp(-jnp.abs(z))
    inv = 1.0 / (1.0 + e)
    einv = e * inv
    sig_pos = jnp.where(z >= 0, inv, einv)
    sig_neg = jnp.where(z >= 0, einv, inv)
    fc = jnp.maximum(lb + (1.0 - lb) * sig_pos, F_MIN)
    kk = (1.0 - lb) * sig_neg

    a = _dot(tmat, _split3(jnp.log2(fc)))
    xa = jnp.exp2(a)
    odd = _odd_rows(1)
    return dict(
        qa=(q * xa).astype(BF16),
        kd=(kk * jnp.exp2(a[c - 1:c] - a)).astype(BF16),
        dec=xa[c - 1:c, :],
        ws=[_pair_operand(q, kk, _level_decay(a, m), m).astype(BF16) for m in LEVELS],
        q2=jnp.where(odd, _rows3(q * fc), _rows3(q)).reshape(c, A_WIDTH).astype(BF16),
        k2=jnp.where(odd, _rows3(kk / fc), _rows3(kk)).reshape(c, A_WIDTH).astype(BF16),
    )


def _head_pair(x, pi):
    return x[:, 2 * pi * HEAD_DIM:2 * (pi + 1) * HEAD_DIM]


def _block_diag(x2):
    zero = jnp.zeros((x2.shape[0], HEAD_DIM), x2.dtype)
    return jnp.concatenate([jnp.concatenate([x2[:, :HEAD_DIM], zero], axis=1),
                            jnp.concatenate([zero, x2[:, HEAD_DIM:]], axis=1)], axis=0)


def _hgrn2_products(ops, vb, states):
    new_states, scores, inter = [], [], []
    for hd in range(N_HEADS):
        sl = slice(hd * HEAD_DIM, (hd + 1) * HEAD_DIM)
        new_states.append(states[hd] * ops["dec"][:, sl] + _dot_tn(vb[:, sl], ops["kd"][:, sl]))
    for pi in range(N_HEADS // 2):
        pair_scores = [_dot_nt(_head_pair(ops["q2"], pi), _block_diag(_head_pair(ops["k2"], pi)))]
        for w in ops["ws"]:
            pair_scores.append(_dot_nt(_head_pair(w, pi), _block_diag(_head_pair(w, pi))))
        scores.append(pair_scores)
    for hd in range(N_HEADS):
        sl = slice(hd * HEAD_DIM, (hd + 1) * HEAD_DIM)
        inter.append(_dot(ops["qa"][:, sl], states[hd].T.astype(BF16)))
    return new_states, scores, inter


def _hgrn2_outputs(scores, inter, vb, g, hnorm, masks):
    outs = []
    for pi in range(N_HEADS // 2):
        p = jnp.where(masks[0], scores[pi][0], 0.0)
        for i in range(len(LEVELS)):
            p = jnp.where(masks[1 + i], scores[pi][1 + i], p)
        o2 = _dot(p.astype(BF16), _block_diag(_head_pair(vb, pi)))
        for j in range(2):
            hd = 2 * pi + j
            sl = slice(hd * HEAD_DIM, (hd + 1) * HEAD_DIM)
            o = inter[hd] + o2[:, j * HEAD_DIM:(j + 1) * HEAD_DIM]
            o = o * lax.rsqrt(jnp.mean(o * o, axis=-1, keepdims=True) + EPS) * hnorm[:, sl]
            gh = g[:, sl]
            outs.append((o * (gh * jax.nn.sigmoid(gh))).astype(BF16))
    return jnp.concatenate(outs, axis=1)


def _gelu_tanh(x):
    c1 = math.sqrt(2.0 / math.pi)
    return x * (0.5 + 0.5 * jnp.tanh(x * (c1 + (c1 * 0.044715) * (x * x))))


def _mixer_kernel(x_ref, gmix_ref, win_hbm, lbp_ref, hnorm_ref, lng_ref, lnb_ref, wsp_ref, bsp_ref, wout_hbm,
                  tmat_ref, lvl_ref, o_ref, state_ref, mix_ref, win_ref, wout_ref, stage_in, sem_in, stage_out,
                  sem_out, *, layer):
    c = RECUR_CHUNK
    n_chunks = MIX_TILE // c

    @pl.when((pl.program_id(0) == 0) & (pl.program_id(1) == 0))
    def _():
        _load_weights([(win_hbm.at[layer], win_ref, stage_in, sem_in),
                       (wout_hbm.at[layer], wout_ref, stage_out, sem_out)])

    @pl.when(pl.program_id(1) == 0)
    def _():
        state_ref[...] = jnp.zeros(state_ref.shape, F32)

    halves = [slice(i * MIX_TILE // 2, (i + 1) * MIX_TILE // 2) for i in range(2)]
    xs = [x_ref[0, r, :] for r in halves]
    hs = [_rmsnorm(xh, gmix_ref[...]).astype(BF16) for xh in xs]
    h = jnp.concatenate(hs, axis=0)
    proj = {}

    def project(k, j, by_halves=False):
        lo = k * A_WIDTH + j * PROJ_COLS
        w = win_ref[:, lo:lo + PROJ_COLS]
        proj[k, j] = jnp.concatenate([_dot(hh, w) for hh in hs], axis=0) if by_halves else _dot(h, w)

    def projected(k, rows):
        return jnp.concatenate([proj[k, j][rows] for j in range(A_WIDTH // PROJ_COLS)], axis=1)

    lp = lbp_ref[...]
    ex = jnp.exp(lp - jnp.max(lp, axis=0, keepdims=True))
    prob = ex / jnp.sum(ex, axis=0, keepdims=True)
    lbs = jnp.sum(prob[:layer + 1, :], axis=0, keepdims=True) - prob[0:1, :]
    hnorm = hnorm_ref[...]
    tmat = tmat_ref[...]
    lvl = lvl_ref[...]
    masks = [lvl == i for i in range(1 + len(LEVELS))]

    p = SPATIAL_CHUNK
    prow = lax.broadcasted_iota(jnp.int32, (p, p), 0)
    pcol = lax.broadcasted_iota(jnp.int32, (p, p), 1)
    w_tril = [jnp.where(prow >= pcol, wsp_ref[gi], 0.0).astype(BF16) for gi in range(N_GROUPS)]
    lng = lng_ref[...]
    lnb = lnb_ref[...]
    bias = bsp_ref[...]

    def spatial_gates(si):
        rows = slice(si * p, (si + 1) * p)
        u = _gelu_tanh(projected(4, rows))
        vv = _gelu_tanh(projected(5, rows))
        mu = jnp.mean(vv, axis=-1, keepdims=True)
        var = jnp.mean(jnp.square(vv - mu), axis=-1, keepdims=True)
        return u, ((vv - mu) * lax.rsqrt(var + EPS) * lng + lnb).astype(BF16)

    def spatial_mix(si, u, vn):
        rows = slice(si * p, (si + 1) * p)
        for gi in range(N_GROUPS):
            lo, hi = gi * GROUP_DIM, (gi + 1) * GROUP_DIM
            s = _dot(w_tril[gi], vn[:, lo:hi]) + bias[:, lo:hi]
            mix_ref[rows, A_WIDTH + lo:A_WIDTH + hi] = (u[:, lo:hi] * s).astype(BF16)

    def field(ci, k):
        return projected(k, slice(ci * c, (ci + 1) * c))

    operands, products = {}, {}

    def prepare(ci):
        operands[ci] = _hgrn2_operands(field(ci, 0), field(ci, 1), lbs, tmat)

    def finish(ci):
        scores, inter, vb = products.pop(ci)
        mix_ref[ci * c:(ci + 1) * c, 0:A_WIDTH] = _hgrn2_outputs(scores, inter, vb, field(ci, 3), hnorm, masks)

    project(1, 0, by_halves=True)
    project(1, 1)
    project(0, 0)
    project(0, 1)
    slots = ((5, 0), (5, 1), (4, 0), (4, 1), (2, 0), (2, 1), (3, 0), (3, 1))
    for i, (k, j) in enumerate(slots):
        project(k, j)
        for ci in range(i * n_chunks // len(slots), (i + 1) * n_chunks // len(slots)):
            prepare(ci)

    states = [state_ref[hd] for hd in range(N_HEADS)]
    for ci in range(n_chunks):
        vb = field(ci, 2).astype(BF16)
        states, scores, inter = _hgrn2_products(operands.pop(ci), vb, states)
        products[ci] = (scores, inter, vb)
        if ci >= 1:
            finish(ci - 1)
        if ci % (p // c) == 0:
            spatial_mix(ci // (p // c), *spatial_gates(ci // (p // c)))
    finish(n_chunks - 1)
    for hd in range(N_HEADS):
        state_ref[hd] = states[hd]

    for r, xh in zip(halves, xs):
        o_ref[0, r, :] = xh + _dot(mix_ref[r, :], wout_ref[...])


def _mixer(x, gmix, win, lbp, hnorm, lng, lnb, wsp, bsp, wout, *, layer):
    bsz, seq, _ = x.shape
    c = RECUR_CHUNK
    tile = pl.BlockSpec((1, MIX_TILE, D_MODEL), lambda b, s: (b, s, 0))
    tmat = jnp.asarray(_prefix_sum_matrix(), BF16)
    lvl = jnp.asarray(_level_ids())
    return pl.pallas_call(
        functools.partial(_mixer_kernel, layer=layer),
        grid=(bsz, seq // MIX_TILE),
        in_specs=[tile, _resident((1, D_MODEL)), _hbm(), _resident((DEPTH, A_WIDTH)),
                  _resident((1, A_WIDTH)), _resident((1, B_WIDTH)), _resident((1, B_WIDTH)),
                  _resident((N_GROUPS, SPATIAL_CHUNK, SPATIAL_CHUNK)), _resident((SPATIAL_CHUNK, B_WIDTH)),
                  _hbm(), _resident((c, N_SPLIT * c)), _resident((c, 2 * c))],
        out_specs=tile,
        out_shape=jax.ShapeDtypeStruct(x.shape, F32),
        scratch_shapes=[pltpu.VMEM((N_HEADS, HEAD_DIM, HEAD_DIM), F32),
                        pltpu.VMEM((MIX_TILE, D_MODEL), BF16),
                        pltpu.VMEM((D_MODEL, D_IN), BF16), pltpu.VMEM((D_MODEL, D_MODEL), BF16),
                        *_stage(D_MODEL, D_IN, MIX_STAGE_SLOTS), *_stage(D_MODEL, D_MODEL, MIX_STAGE_SLOTS)],
        compiler_params=pltpu.CompilerParams(dimension_semantics=("arbitrary", "arbitrary"),
                                             vmem_limit_bytes=VMEM_LIMIT_BYTES),
        name="mixer",
    )(x, gmix, win, lbp, hnorm, lng, lnb, wsp, bsp, wout, tmat, lvl)


def kernel(x, norm_ffn1, ffn1_w_gate, ffn1_w_up, ffn1_w_down, norm_mix, w_in, lb_param, hgrn_norm, ln_v_gain,
           ln_v_bias, w_spatial, b_spatial, w_out, norm_ffn2, ffn2_w_gate, ffn2_w_up, ffn2_w_down, norm_final):
    bsz, seq, d = x.shape
    row = lambda t: t.reshape(1, -1).astype(F32)
    lbp = lb_param.astype(F32)
    gf = row(norm_final)
    for l in range(DEPTH):
        x = _ffn(x.reshape(bsz * seq, d), row(norm_ffn1[l]), ffn1_w_gate, ffn1_w_up, ffn1_w_down, gf,
                 layer=l, final=False).reshape(bsz, seq, d)
        bsp = jnp.repeat(b_spatial[l].astype(F32).T, GROUP_DIM, axis=1)
        x = _mixer(x, row(norm_mix[l]), w_in, lbp, row(hgrn_norm[l]), row(ln_v_gain[l]), row(ln_v_bias[l]),
                   w_spatial[l].astype(F32), bsp, w_out, layer=l)
        x = _ffn(x.reshape(bsz * seq, d), row(norm_ffn2[l]), ffn2_w_gate, ffn2_w_up, ffn2_w_down, gf,
                 layer=l, final=(l == DEPTH - 1)).reshape(bsz, seq, d)
    return x
```

```python
import functools
import math

import jax
import jax.numpy as jnp
import numpy as np
from jax import lax
from jax.experimental import pallas as pl
from jax.experimental.pallas import tpu as pltpu

D_MODEL = 1024
DEPTH = 2
A_WIDTH = 512
HEAD_DIM = 128
N_HEADS = A_WIDTH // HEAD_DIM
B_WIDTH = 512
N_GROUPS = 4
GROUP_DIM = B_WIDTH // N_GROUPS
D_IN = 4 * A_WIDTH + 2 * B_WIDTH
D_FF = int(math.ceil(8 * D_MODEL / 3 / 128)) * 128
SPATIAL_CHUNK = 128
RECUR_CHUNK = 64
FFN_RES = 0.5
EPS = 1e-6
F_MIN = 1e-20

SUBLANES = 8
LEVELS = (32, 16, 8, 4, 2)
N_SPLIT = 3
FFN_TILE = 1024
FFN_ROWS = 256
FF_BLOCK = 256
STAGE_ELEMS = 3 * 64 * 1024
FFN_STAGE_SLOTS = 8
MIX_STAGE_SLOTS = 6
MIX_TILE = 1024
PROJ_COLS = 256
VMEM_LIMIT_BYTES = 56 * 1024 * 1024

F32 = jnp.float32
BF16 = jnp.bfloat16


def _rmsnorm(x, g):
    return x * lax.rsqrt(jnp.mean(x * x, axis=-1, keepdims=True) + EPS) * g


def _dot(a, b):
    return jnp.dot(a, b, preferred_element_type=F32)


def _dot_nt(a, b):
    return lax.dot_general(a, b, (((1,), (1,)), ((), ())), preferred_element_type=F32)


def _dot_tn(a, b):
    return lax.dot_general(a, b, (((0,), (0,)), ((), ())), preferred_element_type=F32)


def _hbm():
    return pl.BlockSpec(memory_space=pl.ANY)


def _stage(rows, cols, slots):
    n = max(r for r in range(SUBLANES, rows + 1, SUBLANES) if rows % r == 0 and r * cols <= STAGE_ELEMS)
    return [pltpu.VMEM((slots, n, cols), F32), pltpu.SemaphoreType.DMA((slots,))]


def _load_weights(jobs):
    chunks = [(src, dst, stage, sems, r0)
              for src, dst, stage, sems in jobs for r0 in range(0, dst.shape[0], stage.shape[1])]
    slots = jobs[0][2].shape[0]
    assert all(stage.shape[0] == slots for _, _, stage, _ in jobs)
    ahead = slots - 1

    def copy(i):
        src, _, stage, sems, r0 = chunks[i]
        slot = i % slots
        return pltpu.make_async_copy(src.at[pl.ds(r0, stage.shape[1]), :], stage.at[slot], sems.at[slot])

    for i in range(min(ahead, len(chunks))):
        copy(i).start()
    for i, (_, dst, stage, _, r0) in enumerate(chunks):
        if i + ahead < len(chunks):
            copy(i + ahead).start()
        copy(i).wait()
        dst[r0:r0 + stage.shape[1], :] = stage[i % slots].astype(BF16)


def _ffn_kernel(x_ref, g_ref, wg_hbm, wu_hbm, wd_hbm, gf_ref, o_ref, wg_ref, wu_ref, wd_ref, stage_in, sem_in,
                stage_out, sem_out, *, layer, final):
    @pl.when(pl.program_id(0) == 0)
    def _():
        _load_weights([(wg_hbm.at[layer], wg_ref, stage_in, sem_in), (wu_hbm.at[layer], wu_ref, stage_in, sem_in),
                       (wd_hbm.at[layer], wd_ref, stage_out, sem_out)])

    n_groups = FFN_TILE // FFN_ROWS
    n_blocks = D_FF // FF_BLOCK
    for i in range(n_groups):
        rows = slice(i * FFN_ROWS, (i + 1) * FFN_ROWS)
        h = _rmsnorm(x_ref[rows, :], g_ref[...]).astype(BF16)

        def gate_up(j):
            cols = slice(j * FF_BLOCK, (j + 1) * FF_BLOCK)
            return _dot(h, wg_ref[:, cols]), _dot(h, wu_ref[:, cols])

        acc = None
        pending = gate_up(0)
        for j in range(n_blocks):
            nxt = gate_up(j + 1) if j + 1 < n_blocks else None
            gate, up = pending
            act = (gate * jax.nn.sigmoid(gate) * up).astype(BF16)
            part = _dot(act, wd_ref[j * FF_BLOCK:(j + 1) * FF_BLOCK, :])
            acc = part if acc is None else acc + part
            pending = nxt
        out = x_ref[rows, :] + FFN_RES * acc
        if final:
            out = _rmsnorm(out, gf_ref[...])
        o_ref[rows, :] = out


def _resident(shape):
    return pl.BlockSpec(shape, lambda *_: (0,) * len(shape), pipeline_mode=pl.Buffered(1))


def _ffn(x2d, g, wg, wu, wd, gf, *, layer, final):
    m = x2d.shape[0]
    tile = pl.BlockSpec((FFN_TILE, D_MODEL), lambda i: (i, 0))
    return pl.pallas_call(
        functools.partial(_ffn_kernel, layer=layer, final=final),
        grid=(m // FFN_TILE,),
        in_specs=[tile, _resident((1, D_MODEL)), _hbm(), _hbm(), _hbm(), _resident((1, D_MODEL))],
        out_specs=tile,
        out_shape=jax.ShapeDtypeStruct(x2d.shape, F32),
        scratch_shapes=[pltpu.VMEM((D_MODEL, D_FF), BF16), pltpu.VMEM((D_MODEL, D_FF), BF16),
                        pltpu.VMEM((D_FF, D_MODEL), BF16),
                        *_stage(D_MODEL, D_FF, FFN_STAGE_SLOTS), *_stage(D_FF, D_MODEL, FFN_STAGE_SLOTS)],
        compiler_params=pltpu.CompilerParams(dimension_semantics=("arbitrary",),
                                             vmem_limit_bytes=VMEM_LIMIT_BYTES),
        name="ffn_final" if final else "ffn",
    )(x2d, g, wg, wu, wd, gf)


def _split3(x):
    hi = x.astype(BF16)
    r1 = x - hi.astype(F32)
    mid = r1.astype(BF16)
    lo = (r1 - mid.astype(F32)).astype(BF16)
    return jnp.concatenate([hi, mid, lo], axis=0)


def _prefix_sum_matrix():
    c = RECUR_CHUNK
    tril = (np.arange(c)[None, :] <= np.arange(c)[:, None]).astype(np.float32)
    return np.concatenate([tril] * N_SPLIT, axis=1)


def _level_ids():
    c = RECUR_CHUNK
    t = np.arange(c)[:, None]
    s = np.arange(c)[None, :]
    ids = np.full((c, c), -1, np.int32)
    ids[((t // 2) == (s // 2)) & (s <= t)] = 0
    for i, m in enumerate(LEVELS):
        pair = ((t // (2 * m)) == (s // (2 * m))) & ((t // m) % 2 == 1) & ((s // m) % 2 == 0)
        ids[pair] = 1 + i
    return np.concatenate([ids, ids], axis=1)


def _rows3(x):
    return x.reshape(RECUR_CHUNK // SUBLANES, SUBLANES, A_WIDTH)


def _odd_rows(m):
    return (lax.broadcasted_iota(jnp.int32, (1, SUBLANES, A_WIDTH), 1) // m) % 2 == 1


def _pair_operand(q, kk, x, m):
    c = RECUR_CHUNK
    if m >= SUBLANES:
        parts = [(q if b % 2 else kk)[b * m:(b + 1) * m] * x[b * m:(b + 1) * m] for b in range(c // m)]
        return jnp.concatenate(parts, axis=0)
    return (jnp.where(_odd_rows(m), _rows3(q), _rows3(kk)) * _rows3(x)).reshape(c, A_WIDTH)


def _level_decay(a, m):
    c = RECUR_CHUNK
    if m >= SUBLANES:
        parts = []
        for j in range(c // (2 * m)):
            ref = a[(2 * j + 1) * m - 1:(2 * j + 1) * m]
            parts += [ref - a[2 * j * m:(2 * j + 1) * m], a[(2 * j + 1) * m:(2 * j + 2) * m] - ref]
        return jnp.exp2(jnp.concatenate(parts, axis=0))
    a3 = _rows3(a)
    row = lax.broadcasted_iota(jnp.int32, (1, SUBLANES, A_WIDTH), 1)
    ref = a3[:, m - 1:m, :]
    for j in range(1, SUBLANES // (2 * m)):
        ref = jnp.where(row >= 2 * j * m, a3[:, (2 * j + 1) * m - 1:(2 * j + 1) * m, :], ref)
    sign = jnp.where(_odd_rows(m), 1.0, -1.0)
    return jnp.exp2((a3 - ref) * sign).reshape(c, A_WIDTH)


def _hgrn2_operands(q, z, lb, tmat):
    c = RECUR_CHUNK
    e = jnp.exp(-jnp.abs(z))
    inv = 1.0 / (1.0 + e)
    einv = e * inv
    sig_pos = jnp.where(z >= 0, inv, einv)
    sig_neg = jnp.where(z >= 0, einv, inv)
    fc = jnp.maximum(lb + (1.0 - lb) * sig_pos, F_MIN)
    kk = (1.0 - lb) * sig_neg

    a = _dot(tmat, _split3(jnp.log2(fc)))
    xa = jnp.exp2(a)
    odd = _odd_rows(1)
    return dict(
        qa=(q * xa).astype(BF16),
        kd=(kk * jnp.exp2(a[c - 1:c] - a)).astype(BF16),
        dec=xa[c - 1:c, :],
        ws=[_pair_operand(q, kk, _level_decay(a, m), m).astype(BF16) for m in LEVELS],
        q2=jnp.where(odd, _rows3(q * fc), _rows3(q)).reshape(c, A_WIDTH).astype(BF16),
        k2=jnp.where(odd, _rows3(kk / fc), _rows3(kk)).reshape(c, A_WIDTH).astype(BF16),
    )


def _head_pair(x, pi):
    return x[:, 2 * pi * HEAD_DIM:2 * (pi + 1) * HEAD_DIM]


def _block_diag(x2):
    zero = jnp.zeros((x2.shape[0], HEAD_DIM), x2.dtype)
    return jnp.concatenate([jnp.concatenate([x2[:, :HEAD_DIM], zero], axis=1),
                            jnp.concatenate([zero, x2[:, HEAD_DIM:]], axis=1)], axis=0)


def _hgrn2_products(ops, vb, states):
    new_states, scores, inter = [], [], []
    for hd in range(N_HEADS):
        sl = slice(hd * HEAD_DIM, (hd + 1) * HEAD_DIM)
        new_states.append(states[hd] * ops["dec"][:, sl] + _dot_tn(vb[:, sl], ops["kd"][:, sl]))
    for pi in range(N_HEADS // 2):
        pair_scores = [_dot_nt(_head_pair(ops["q2"], pi), _block_diag(_head_pair(ops["k2"], pi)))]
        for w in ops["ws"]:
            pair_scores.append(_dot_nt(_head_pair(w, pi), _block_diag(_head_pair(w, pi))))
        scores.append(pair_scores)
    for hd in range(N_HEADS):
        sl = slice(hd * HEAD_DIM, (hd + 1) * HEAD_DIM)
        inter.append(_dot(ops["qa"][:, sl], states[hd].T.astype(BF16)))
    return new_states, scores, inter


def _hgrn2_outputs(scores, inter, vb, g, hnorm, masks):
    outs = []
    for pi in range(N_HEADS // 2):
        p = jnp.where(masks[0], scores[pi][0], 0.0)
        for i in range(len(LEVELS)):
            p = jnp.where(masks[1 + i], scores[pi][1 + i], p)
        o2 = _dot(p.astype(BF16), _block_diag(_head_pair(vb, pi)))
        for j in range(2):
            hd = 2 * pi + j
            sl = slice(hd * HEAD_DIM, (hd + 1) * HEAD_DIM)
            o = inter[hd] + o2[:, j * HEAD_DIM:(j + 1) * HEAD_DIM]
            o = o * lax.rsqrt(jnp.mean(o * o, axis=-1, keepdims=True) + EPS) * hnorm[:, sl]
            gh = g[:, sl]
            outs.append((o * (gh * jax.nn.sigmoid(gh))).astype(BF16))
    return jnp.concatenate(outs, axis=1)


def _gelu_tanh(x):
    c1 = math.sqrt(2.0 / math.pi)
    return x * (0.5 + 0.5 * jnp.tanh(x * (c1 + (c1 * 0.044715) * (x * x))))


def _mixer_kernel(x_ref, gmix_ref, win_hbm, lbp_ref, hnorm_ref, lng_ref, lnb_ref, wsp_ref, bsp_ref, wout_hbm,
                  tmat_ref, lvl_ref, o_ref, state_ref, mix_ref, win_ref, wout_ref, stage_in, sem_in, stage_out,
                  sem_out, *, layer):
    c = RECUR_CHUNK
    n_chunks = MIX_TILE // c

    @pl.when((pl.program_id(0) == 0) & (pl.program_id(1) == 0))
    def _():
        _load_weights([(win_hbm.at[layer], win_ref, stage_in, sem_in),
                       (wout_hbm.at[layer], wout_ref, stage_out, sem_out)])

    @pl.when(pl.program_id(1) == 0)
    def _():
        state_ref[...] = jnp.zeros(state_ref.shape, F32)

    halves = [slice(i * MIX_TILE // 2, (i + 1) * MIX_TILE // 2) for i in range(2)]
    xs = [x_ref[0, r, :] for r in halves]
    hs = [_rmsnorm(xh, gmix_ref[...]).astype(BF16) for xh in xs]
    h = jnp.concatenate(hs, axis=0)
    proj = {}

    def project(k, j, by_halves=False):
        lo = k * A_WIDTH + j * PROJ_COLS
        w = win_ref[:, lo:lo + PROJ_COLS]
        proj[k, j] = jnp.concatenate([_dot(hh, w) for hh in hs], axis=0) if by_halves else _dot(h, w)

    def projected(k, rows):
        return jnp.concatenate([proj[k, j][rows] for j in range(A_WIDTH // PROJ_COLS)], axis=1)

    lp = lbp_ref[...]
    ex = jnp.exp(lp - jnp.max(lp, axis=0, keepdims=True))
    prob = ex / jnp.sum(ex, axis=0, keepdims=True)
    lbs = jnp.sum(prob[:layer + 1, :], axis=0, keepdims=True) - prob[0:1, :]
    hnorm = hnorm_ref[...]
    tmat = tmat_ref[...]
    lvl = lvl_ref[...]
    masks = [lvl == i for i in range(1 + len(LEVELS))]

    p = SPATIAL_CHUNK
    prow = lax.broadcasted_iota(jnp.int32, (p, p), 0)
    pcol = lax.broadcasted_iota(jnp.int32, (p, p), 1)
    w_tril = [jnp.where(prow >= pcol, wsp_ref[gi], 0.0).astype(BF16) for gi in range(N_GROUPS)]
    lng = lng_ref[...]
    lnb = lnb_ref[...]
    bias = bsp_ref[...]

    def spatial_gates(si):
        rows = slice(si * p, (si + 1) * p)
        u = _gelu_tanh(projected(4, rows))
        vv = _gelu_tanh(projected(5, rows))
        mu = jnp.mean(vv, axis=-1, keepdims=True)
        var = jnp.mean(jnp.square(vv - mu), axis=-1, keepdims=True)
        return u, ((vv - mu) * lax.rsqrt(var + EPS) * lng + lnb).astype(BF16)

    def spatial_mix(si, u, vn):
        rows = slice(si * p, (si + 1) * p)
        for gi in range(N_GROUPS):
            lo, hi = gi * GROUP_DIM, (gi + 1) * GROUP_DIM
            s = _dot(w_tril[gi], vn[:, lo:hi]) + bias[:, lo:hi]
            mix_ref[rows, A_WIDTH + lo:A_WIDTH + hi] = (u[:, lo:hi] * s).astype(BF16)

    def field(ci, k):
        return projected(k, slice(ci * c, (ci + 1) * c))

    operands, products = {}, {}

    def prepare(ci):
        operands[ci] = _hgrn2_operands(field(ci, 0), field(ci, 1), lbs, tmat)

    def finish(ci):
        scores, inter, vb = products.pop(ci)
        mix_ref[ci * c:(ci + 1) * c, 0:A_WIDTH] = _hgrn2_outputs(scores, inter, vb, field(ci, 3), hnorm, masks)

    project(1, 0, by_halves=True)
    project(1, 1)
    project(0, 0)
    project(0, 1)
    slots = ((5, 0), (5, 1), (4, 0), (4, 1), (2, 0), (2, 1), (3, 0), (3, 1))
    for i, (k, j) in enumerate(slots):
        project(k, j)
        for ci in range(i * n_chunks // len(slots), (i + 1) * n_chunks // len(slots)):
            prepare(ci)

    states = [state_ref[hd] for hd in range(N_HEADS)]
    for ci in range(n_chunks):
        vb = field(ci, 2).astype(BF16)
        states, scores, inter = _hgrn2_products(operands.pop(ci), vb, states)
        products[ci] = (scores, inter, vb)
        if ci >= 1:
            finish(ci - 1)
        if ci % (p // c) == 0:
            spatial_mix(ci // (p // c), *spatial_gates(ci // (p // c)))
    finish(n_chunks - 1)
    for hd in range(N_HEADS):
        state_ref[hd] = states[hd]

    for r, xh in zip(halves, xs):
        o_ref[0, r, :] = xh + _dot(mix_ref[r, :], wout_ref[...])


def _mixer(x, gmix, win, lbp, hnorm, lng, lnb, wsp, bsp, wout, *, layer):
    bsz, seq, _ = x.shape
    c = RECUR_CHUNK
    tile = pl.BlockSpec((1, MIX_TILE, D_MODEL), lambda b, s: (b, s, 0))
    tmat = jnp.asarray(_prefix_sum_matrix(), BF16)
    lvl = jnp.asarray(_level_ids())
    return pl.pallas_call(
        functools.partial(_mixer_kernel, layer=layer),
        grid=(bsz, seq // MIX_TILE),
        in_specs=[tile, _resident((1, D_MODEL)), _hbm(), _resident((DEPTH, A_WIDTH)),
                  _resident((1, A_WIDTH)), _resident((1, B_WIDTH)), _resident((1, B_WIDTH)),
                  _resident((N_GROUPS, SPATIAL_CHUNK, SPATIAL_CHUNK)), _resident((SPATIAL_CHUNK, B_WIDTH)),
                  _hbm(), _resident((c, N_SPLIT * c)), _resident((c, 2 * c))],
        out_specs=tile,
        out_shape=jax.ShapeDtypeStruct(x.shape, F32),
        scratch_shapes=[pltpu.VMEM((N_HEADS, HEAD_DIM, HEAD_DIM), F32),
                        pltpu.VMEM((MIX_TILE, D_MODEL), BF16),
                        pltpu.VMEM((D_MODEL, D_IN), BF16), pltpu.VMEM((D_MODEL, D_MODEL), BF16),
                        *_stage(D_MODEL, D_IN, MIX_STAGE_SLOTS), *_stage(D_MODEL, D_MODEL, MIX_STAGE_SLOTS)],
        compiler_params=pltpu.CompilerParams(dimension_semantics=("arbitrary", "arbitrary"),
                                             vmem_limit_bytes=VMEM_LIMIT_BYTES),
        name="mixer",
    )(x, gmix, win, lbp, hnorm, lng, lnb, wsp, bsp, wout, tmat, lvl)


def kernel(x, norm_ffn1, ffn1_w_gate, ffn1_w_up, ffn1_w_down, norm_mix, w_in, lb_param, hgrn_norm, ln_v_gain,
           ln_v_bias, w_spatial, b_spatial, w_out, norm_ffn2, ffn2_w_gate, ffn2_w_up, ffn2_w_down, norm_final):
    bsz, seq, d = x.shape
    row = lambda t: t.reshape(1, -1).astype(F32)
    lbp = lb_param.astype(F32)
    gf = row(norm_final)
    for l in range(DEPTH):
        x = _ffn(x.reshape(bsz * seq, d), row(norm_ffn1[l]), ffn1_w_gate, ffn1_w_up, ffn1_w_down, gf,
                 layer=l, final=False).reshape(bsz, seq, d)
        bsp = jnp.repeat(b_spatial[l].astype(F32).T, GROUP_DIM, axis=1)
        x = _mixer(x, row(norm_mix[l]), w_in, lbp, row(hgrn_norm[l]), row(ln_v_gain[l]), row(ln_v_bias[l]),
                   w_spatial[l].astype(F32), bsp, w_out, layer=l)
        x = _ffn(x.reshape(bsz * seq, d), row(norm_ffn2[l]), ffn2_w_gate, ffn2_w_up, ffn2_w_down, gf,
                 layer=l, final=(l == DEPTH - 1)).reshape(bsz, seq, d)
    return x
```
